```python
import math
import jax, jax.numpy as jnp
from jax import lax
import numpy as np

D_MODEL = 2048
BATCH = 4
SEQ = 4096
DEPTH = 4

N_A = DEPTH // 2
N_B = DEPTH - N_A
N_META = 16
CONV_WIDTH = 3
HEAD_DIM = 128
N_HEADS = D_MODEL // HEAD_DIM
D_FF = ((8 * D_MODEL // 3 + 255) // 256) * 256
BLOCK = 128
PAD = BLOCK - N_META
EPS = 1e-6
NEG = -1e30

kernel_name = "yoco_shortconv_forgetting_attn_meta"


def rms_norm(x, g):
    xf = x.astype(jnp.float32)
    y = xf * lax.rsqrt(jnp.mean(xf * xf, axis=-1, keepdims=True) + EPS)
    return (y * g.astype(jnp.float32)).astype(x.dtype)


def short_conv_mixer(xn, w_in, conv_w, w_out):
    L = xn.shape[1]
    b_gate, c_gate, h = jnp.split(xn @ w_in, 3, axis=-1)
    u = jnp.pad(c_gate * h, ((0, 0), (CONV_WIDTH - 1, 0), (0, 0)))
    conv = sum(u[:, j:j + L, :] * conv_w[j] for j in range(CONV_WIDTH))
    return (b_gate * conv) @ w_out


def swiglu(xn, w_gu, w_down):
    g, u = jnp.split(xn @ w_gu, 2, axis=-1)
    return (jax.nn.silu(g) * u) @ w_down


def shared_kv(h, kv_norm, w_kv, k_norm, w_f, b_f):
    Bsz, L, _ = h.shape
    xn = rms_norm(h, kv_norm)
    k, v = jnp.split(xn @ w_kv, 2, axis=-1)
    k = rms_norm(k.reshape(Bsz, L, N_HEADS, HEAD_DIM), k_norm)
    v = v.reshape(Bsz, L, N_HEADS, HEAD_DIM)
    log_f = jax.nn.log_sigmoid((xn @ w_f + b_f).astype(jnp.float32))
    pad4 = ((0, 0), (PAD, 0), (0, 0), (0, 0))
    k = jnp.pad(k, pad4)
    v = jnp.pad(v, pad4)
    c = jnp.cumsum(jnp.pad(log_f, ((0, 0), (PAD, 0), (0, 0))), axis=1)
    return k, v, jnp.transpose(c, (0, 2, 1))


def forgetting_attention(xn, w_q, q_norm, w_o, k, v, c):
    Bsz, L, _ = xn.shape
    Lp = k.shape[1]
    q = rms_norm((xn @ w_q).reshape(Bsz, L, N_HEADS, HEAD_DIM), q_norm)
    q = jnp.pad(q, ((0, 0), (PAD, 0), (0, 0), (0, 0)))
    scale = 1.0 / math.sqrt(HEAD_DIM)
    kpos = jnp.arange(Lp)

    def block(i):
        start = i * BLOCK
        qb = lax.dynamic_slice_in_dim(q, start, BLOCK, axis=1)
        cq = lax.dynamic_slice_in_dim(c, start, BLOCK, axis=2)
        s = jnp.einsum('bqhd,bkhd->bhqk', qb, k).astype(jnp.float32) * scale
        s = s + (cq[..., :, None] - c[..., None, :])
        qpos = start + jnp.arange(BLOCK)
        mask = (kpos[None, :] <= qpos[:, None]) & (kpos[None, :] >= PAD)
        s = jnp.where(mask, s, NEG)
        p = jax.nn.softmax(s, axis=-1).astype(v.dtype)
        return jnp.einsum('bhqk,bkhd->bqhd', p, v)

    o = lax.map(block, jnp.arange(Lp // BLOCK))
    o = jnp.transpose(o, (1, 0, 2, 3, 4)).reshape(Bsz, Lp, N_HEADS * HEAD_DIM)[:, PAD:]
    return o @ w_o


def setup_inputs(seed: int = 0) -> dict:
    key = jax.random.key(seed)
    ks = jax.random.split(key, 24)
    D, F, H, Dh = D_MODEL, D_FF, N_HEADS, HEAD_DIM
    nrm = lambda k, shape, s: jax.random.normal(k, shape, jnp.float32) * s
    gain = lambda k, shape: 1.0 + nrm(k, shape, 0.02)
    out_s = 0.5 / math.sqrt(DEPTH)
    return {
        "x": nrm(ks[0], (BATCH, SEQ, D), 1.0),
        "meta": nrm(ks[1], (N_META, D), 1.0),
        "a_norm": gain(ks[2], (N_A, D)),
        "a_w_in": nrm(ks[3], (N_A, D, 3 * D), D ** -0.5),
        "a_conv": nrm(ks[4], (N_A, CONV_WIDTH, D), CONV_WIDTH ** -0.5),
        "a_w_out": nrm(ks[5], (N_A, D, D), D ** -0.5 * out_s),
        "kv_norm": gain(ks[6], (D,)),
        "w_kv": nrm(ks[7], (D, 2 * H * Dh), D ** -0.5),
        "k_norm": gain(ks[8], (Dh,)),
        "w_f": nrm(ks[9], (D, H), D ** -0.5),
        "b_f": 3.0 + nrm(ks[10], (H,), 0.5),
        "b_norm": gain(ks[11], (N_B, D)),
        "b_w_q": nrm(ks[12], (N_B, D, H * Dh), D ** -0.5),
        "b_q_norm": gain(ks[13], (N_B, Dh)),
        "b_w_o": nrm(ks[14], (N_B, H * Dh, D), D ** -0.5 * out_s),
        "ffn_norm": gain(ks[15], (DEPTH, D)),
        "ffn_w_gu": nrm(ks[16], (DEPTH, D, 2 * F), D ** -0.5),
        "ffn_w_down": nrm(ks[17], (DEPTH, F, D), F ** -0.5 * out_s),
    }


def reference(x, meta, a_norm, a_w_in, a_conv, a_w_out, kv_norm, w_kv, k_norm, w_f, b_f,
              b_norm, b_w_q, b_q_norm, b_w_o, ffn_norm, ffn_w_gu, ffn_w_down):
    Bsz = x.shape[0]
    meta_b = jnp.broadcast_to(meta.astype(x.dtype)[None], (Bsz, N_META, D_MODEL))
    h = jnp.concatenate([meta_b, x], axis=1)
    k = v = c = None
    for layer in range(DEPTH):
        if layer < N_A:
            h = h + short_conv_mixer(rms_norm(h, a_norm[layer]), a_w_in[layer],
                                     a_conv[layer], a_w_out[layer])
        else:
            if layer == N_A:
                k, v, c = shared_kv(h, kv_norm, w_kv, k_norm, w_f, b_f)
            j = layer - N_A
            h = h + forgetting_attention(rms_norm(h, b_norm[j]), b_w_q[j], b_q_norm[j],
                                         b_w_o[j], k, v, c)
        h = h + swiglu(rms_norm(h, ffn_norm[layer]), ffn_w_gu[layer], ffn_w_down[layer])
    return h[:, N_META:, :]
```

```python
import functools
import math

import jax
import jax.numpy as jnp
from jax import lax
from jax.experimental import pallas as pl
from jax.experimental.pallas import tpu as pltpu

N_META = 16
CONV_WIDTH = 3
HEAD_DIM = 128
BLOCK = 128
PAD = BLOCK - N_META
EPS = 1e-6
NEG = -1e30

F32 = jnp.float32
BF16 = jnp.bfloat16

ROW_TILE = 768
NORM_CHUNK = 32
VMEM_LIMIT_BYTES = 56 * 1024 * 1024


def _params(*semantics):
    return pltpu.CompilerParams(dimension_semantics=semantics,
                                vmem_limit_bytes=VMEM_LIMIT_BYTES)


def _rms_rows(x_ref, g_ref, out_ref):
    rows = x_ref.shape[0]
    g = g_ref[...]

    def body(c, carry):
        r0 = pl.multiple_of(c * NORM_CHUNK, NORM_CHUNK)
        x = x_ref[pl.ds(r0, NORM_CHUNK), :]
        ms = jnp.mean(x * x, axis=-1, keepdims=True)
        out_ref[pl.ds(r0, NORM_CHUNK), :] = (x * lax.rsqrt(ms + EPS) * g).astype(out_ref.dtype)
        return carry

    lax.fori_loop(0, rows // NORM_CHUNK, body, 0)


def _head_rms(x, gain, scale):
    outs = []
    for hh in range(x.shape[1] // HEAD_DIM):
        xh = x[:, hh * HEAD_DIM:(hh + 1) * HEAD_DIM]
        ms = jnp.mean(xh * xh, axis=-1, keepdims=True)
        y = xh * lax.rsqrt(ms + EPS) * gain
        if scale is not None:
            y = y * scale
        outs.append(y)
    return outs


def _conv_in_kernel(h_ref, g_ref, wb_ref, wc_ref, wh_ref, cw_ref, o_ref,
                    xn_ref, halo_ref, u_ref):
    i = pl.program_id(0)
    j = pl.program_id(1)
    tm = o_ref.shape[0]

    @pl.when(j == 0)
    def _():
        _rms_rows(h_ref, g_ref, xn_ref)

    @pl.when(i == 0)
    def _():
        halo_ref[j] = jnp.zeros(halo_ref.shape[1:], F32)

    xn = xn_ref[...]
    bg = jnp.dot(xn, wb_ref[...], preferred_element_type=F32)
    cg = jnp.dot(xn, wc_ref[...], preferred_element_type=F32)
    hh = jnp.dot(xn, wh_ref[...], preferred_element_type=F32)
    u = cg * hh
    u_ref[0:8, :] = halo_ref[j]
    u_ref[8:, :] = u
    halo_ref[j] = u[tm - 8:, :]
    cw = cw_ref[...]
    conv = (u_ref[pl.ds(6, tm), :] * cw[0:1, :]
            + u_ref[pl.ds(7, tm), :] * cw[1:2, :]
            + u * cw[2:3, :])
    o_ref[...] = (bg * conv).astype(o_ref.dtype)


def _conv_in(h, gain, w_in, conv_w, *, tn=512):
    m, d = h.shape
    tm = ROW_TILE
    nj = d // tn
    return pl.pallas_call(
        _conv_in_kernel,
        grid=(m // tm, nj),
        in_specs=[
            pl.BlockSpec((tm, d), lambda i, j: (i, 0)),
            pl.BlockSpec((1, d), lambda i, j: (0, 0)),
            pl.BlockSpec((d, tn), lambda i, j: (0, j)),
            pl.BlockSpec((d, tn), lambda i, j: (0, nj + j)),
            pl.BlockSpec((d, tn), lambda i, j: (0, 2 * nj + j)),
            pl.BlockSpec((CONV_WIDTH, tn), lambda i, j: (0, j)),
        ],
        out_specs=pl.BlockSpec((tm, tn), lambda i, j: (i, j)),
        out_shape=jax.ShapeDtypeStruct((m, d), BF16),
        scratch_shapes=[
            pltpu.VMEM((tm, d), BF16),
            pltpu.VMEM((nj, 8, tn), F32),
            pltpu.VMEM((tm + 8, tn), F32),
        ],
        compiler_params=_params("arbitrary", "arbitrary"),
        name="conv_in",
    )(h, gain.reshape(1, d), w_in, w_in, w_in, conv_w)


def _mm_res_kernel(a_ref, w_ref, r_ref, o_ref):
    o_ref[...] = r_ref[...] + jnp.dot(a_ref[...], w_ref[...], preferred_element_type=F32)


def _mm_res(a, w, res, *, tn=1024, name="mm_res"):
    m, k = a.shape
    n = w.shape[1]
    tm = ROW_TILE
    return pl.pallas_call(
        _mm_res_kernel,
        grid=(m // tm, n // tn),
        in_specs=[
            pl.BlockSpec((tm, k), lambda i, j: (i, 0)),
            pl.BlockSpec((k, tn), lambda i, j: (0, j)),
            pl.BlockSpec((tm, tn), lambda i, j: (i, j)),
        ],
        out_specs=pl.BlockSpec((tm, tn), lambda i, j: (i, j)),
        out_shape=jax.ShapeDtypeStruct((m, n), F32),
        compiler_params=_params("parallel", "parallel"),
        name=name,
    )(a, w, res)


def _ffn_up_kernel(h_ref, g_ref, wg_ref, wu_ref, o_ref, xn_ref):
    @pl.when(pl.program_id(1) == 0)
    def _():
        _rms_rows(h_ref, g_ref, xn_ref)

    xn = xn_ref[...]
    g = jnp.dot(xn, wg_ref[...], preferred_element_type=F32)
    u = jnp.dot(xn, wu_ref[...], preferred_element_type=F32)
    o_ref[...] = (g * jax.nn.sigmoid(g) * u).astype(o_ref.dtype)


def _ffn_up(h, gain, w_gu, *, tn=512):
    m, d = h.shape
    f = w_gu.shape[1] // 2
    tm = ROW_TILE
    nj = f // tn
    return pl.pallas_call(
        _ffn_up_kernel,
        grid=(m // tm, nj),
        in_specs=[
            pl.BlockSpec((tm, d), lambda i, j: (i, 0)),
            pl.BlockSpec((1, d), lambda i, j: (0, 0)),
            pl.BlockSpec((d, tn), lambda i, j: (0, j)),
            pl.BlockSpec((d, tn), lambda i, j: (0, nj + j)),
        ],
        out_specs=pl.BlockSpec((tm, tn), lambda i, j: (i, j)),
        out_shape=jax.ShapeDtypeStruct((m, f), BF16),
        scratch_shapes=[pltpu.VMEM((tm, d), BF16)],
        compiler_params=_params("arbitrary", "arbitrary"),
        name="ffn_up",
    )(h, gain.reshape(1, d), w_gu, w_gu)


def _kv_kernel(h_ref, g_ref, wk_ref, wv_ref, kn_ref, k_ref, v_ref, xn_ref):
    @pl.when(pl.program_id(1) == 0)
    def _():
        _rms_rows(h_ref, g_ref, xn_ref)

    xn = xn_ref[...]
    k = jnp.dot(xn, wk_ref[...], preferred_element_type=F32)
    v = jnp.dot(xn, wv_ref[...], preferred_element_type=F32)
    for hh, kh in enumerate(_head_rms(k, kn_ref[...], None)):
        k_ref[:, hh * HEAD_DIM:(hh + 1) * HEAD_DIM] = kh.astype(k_ref.dtype)
    v_ref[...] = v.astype(v_ref.dtype)


def _kv_proj(h, gain, w_kv, k_norm, *, tn=512):
    m, d = h.shape
    hd = w_kv.shape[1] // 2
    tm = ROW_TILE
    nj = hd // tn
    return pl.pallas_call(
        _kv_kernel,
        grid=(m // tm, nj),
        in_specs=[
            pl.BlockSpec((tm, d), lambda i, j: (i, 0)),
            pl.BlockSpec((1, d), lambda i, j: (0, 0)),
            pl.BlockSpec((d, tn), lambda i, j: (0, j)),
            pl.BlockSpec((d, tn), lambda i, j: (0, nj + j)),
            pl.BlockSpec((1, HEAD_DIM), lambda i, j: (0, 0)),
        ],
        out_specs=[pl.BlockSpec((tm, tn), lambda i, j: (i, j)),
                   pl.BlockSpec((tm, tn), lambda i, j: (i, j))],
        out_shape=[jax.ShapeDtypeStruct((m, hd), BF16),
                   jax.ShapeDtypeStruct((m, hd), BF16)],
        scratch_shapes=[pltpu.VMEM((tm, d), BF16)],
        compiler_params=_params("arbitrary", "arbitrary"),
        name="kv_proj",
    )(h, gain.reshape(1, d), w_kv, w_kv, k_norm.reshape(1, HEAD_DIM))


def _q_kernel(h_ref, g_ref, wq_ref, qn_ref, q_ref, xn_ref):
    @pl.when(pl.program_id(1) == 0)
    def _():
        _rms_rows(h_ref, g_ref, xn_ref)

    q = jnp.dot(xn_ref[...], wq_ref[...], preferred_element_type=F32)
    scale = 1.0 / math.sqrt(HEAD_DIM)
    for hh, qh in enumerate(_head_rms(q, qn_ref[...], scale)):
        q_ref[:, hh * HEAD_DIM:(hh + 1) * HEAD_DIM] = qh.astype(q_ref.dtype)


def _q_proj(h, gain, w_q, q_norm, *, tn=1024):
    m, d = h.shape
    hd = w_q.shape[1]
    tm = ROW_TILE
    return pl.pallas_call(
        _q_kernel,
        grid=(m // tm, hd // tn),
        in_specs=[
            pl.BlockSpec((tm, d), lambda i, j: (i, 0)),
            pl.BlockSpec((1, d), lambda i, j: (0, 0)),
            pl.BlockSpec((d, tn), lambda i, j: (0, j)),
            pl.BlockSpec((1, HEAD_DIM), lambda i, j: (0, 0)),
        ],
        out_specs=pl.BlockSpec((tm, tn), lambda i, j: (i, j)),
        out_shape=jax.ShapeDtypeStruct((m, hd), BF16),
        scratch_shapes=[pltpu.VMEM((tm, d), BF16)],
        compiler_params=_params("arbitrary", "arbitrary"),
        name="q_proj",
    )(h, gain.reshape(1, d), w_q, q_norm.reshape(1, HEAD_DIM))


def _logf_kernel(h_ref, g_ref, wf_ref, bf_ref, o_ref, *, lp):
    tm = h_ref.shape[0]
    x = h_ref[...]
    ms = jnp.mean(x * x, axis=-1, keepdims=True)
    xn = x * lax.rsqrt(ms + EPS) * g_ref[...]
    z = jnp.dot(xn, wf_ref[...], preferred_element_type=F32,
                precision=lax.Precision.HIGHEST) + bf_ref[...]
    lf = jnp.minimum(z, 0.0) - jnp.log1p(jnp.exp(-jnp.abs(z)))
    row = pl.program_id(0) * tm + lax.broadcasted_iota(jnp.int32, lf.shape, 0)
    o_ref[...] = jnp.where(row % lp >= PAD, lf, 0.0)


def _log_forget(h, gain, w_f, b_f, *, lp, tm=256):
    m, d = h.shape
    nh = w_f.shape[1]
    wf = jnp.zeros((d, HEAD_DIM), F32).at[:, :nh].set(w_f)
    bf = jnp.zeros((1, HEAD_DIM), F32).at[0, :nh].set(b_f)
    return pl.pallas_call(
        functools.partial(_logf_kernel, lp=lp),
        grid=(m // tm,),
        in_specs=[
            pl.BlockSpec((tm, d), lambda i: (i, 0)),
            pl.BlockSpec((1, d), lambda i: (0, 0)),
            pl.BlockSpec((d, HEAD_DIM), lambda i: (0, 0)),
            pl.BlockSpec((1, HEAD_DIM), lambda i: (0, 0)),
        ],
        out_specs=pl.BlockSpec((tm, HEAD_DIM), lambda i: (i, 0)),
        out_shape=jax.ShapeDtypeStruct((m, HEAD_DIM), F32),
        compiler_params=_params("parallel"),
        name="log_forget",
    )(h, gain.reshape(1, d), wf, bf)


def _cumsum_kernel(lf_ref, c_ref, carry_ref):
    ch = lf_ref.shape[0]
    nh = c_ref.shape[1]

    @pl.when(pl.program_id(1) == 0)
    def _():
        carry_ref[...] = jnp.zeros(carry_ref.shape, F32)

    xt = lf_ref[...].T
    s_idx = lax.broadcasted_iota(jnp.int32, (ch, ch), 0)
    t_idx = lax.broadcasted_iota(jnp.int32, (ch, ch), 1)
    tri = (s_idx <= t_idx).astype(F32)
    ct = jnp.dot(xt, tri, preferred_element_type=F32,
                 precision=lax.Precision.HIGHEST) + carry_ref[:, 0:1]
    c_ref[0] = ct[:nh, :]
    carry_ref[...] = jnp.broadcast_to(ct[:, ch - 1:ch], carry_ref.shape)


def _cumsum_heads(lf, *, batch, lp, n_heads, ch=384):
    nch = lp // ch
    return pl.pallas_call(
        _cumsum_kernel,
        grid=(batch, nch),
        in_specs=[pl.BlockSpec((ch, HEAD_DIM), lambda b, t: (b * nch + t, 0))],
        out_specs=pl.BlockSpec((1, n_heads, ch), lambda b, t: (b, 0, t)),
        out_shape=jax.ShapeDtypeStruct((batch, n_heads, lp), F32),
        scratch_shapes=[pltpu.VMEM((HEAD_DIM, HEAD_DIM), F32)],
        compiler_params=_params("arbitrary", "arbitrary"),
        name="cumsum_heads",
    )(lf)


def _attn_kernel(q_ref, k_ref, v_ref, c_ref, o_ref, m_ref, l_ref, acc_ref, *, tk):
    tq = q_ref.shape[0]
    qi = pl.program_id(2)
    m_ref[...] = jnp.full(m_ref.shape, NEG, F32)
    l_ref[...] = jnp.zeros(l_ref.shape, F32)
    acc_ref[...] = jnp.zeros(acc_ref.shape, F32)
    q = q_ref[...]
    q0 = qi * tq

    def body(kj, carry):
        k0 = pl.multiple_of(kj * tk, tk)
        k = k_ref[pl.ds(k0, tk), :]
        v = v_ref[pl.ds(k0, tk), :]
        s = lax.dot_general(q, k, (((1,), (1,)), ((), ())), preferred_element_type=F32)
        s = s - c_ref[0, :, pl.ds(k0, tk)]
        qpos = q0 + lax.broadcasted_iota(jnp.int32, s.shape, 0)
        kpos = k0 + lax.broadcasted_iota(jnp.int32, s.shape, 1)
        s = jnp.where((kpos <= qpos) & (kpos >= PAD), s, NEG)
        m_prev = m_ref[...]
        m_new = jnp.maximum(m_prev, jnp.max(s, axis=-1, keepdims=True))
        p = jnp.exp(s - m_new)
        alpha = jnp.exp(m_prev - m_new)
        l_ref[...] = alpha * l_ref[...] + jnp.sum(p, axis=-1, keepdims=True)
        acc_ref[...] = alpha * acc_ref[...] + jnp.dot(
            p.astype(v.dtype), v, preferred_element_type=F32)
        m_ref[...] = m_new
        return carry

    lax.fori_loop(0, (qi + 1) * (tq // tk), body, 0)
    o_ref[...] = (acc_ref[...] / l_ref[...]).astype(o_ref.dtype)


def _attention(q, k, v, c, *, batch, lp, n_heads, tq=384, tk=384):
    m = q.shape[0]
    nq = lp // tq
    c3 = c.reshape(batch * n_heads, 1, lp)
    return pl.pallas_call(
        functools.partial(_attn_kernel, tk=tk),
        grid=(batch, n_heads, nq),
        in_specs=[
            pl.BlockSpec((tq, HEAD_DIM), lambda b, h, i: (b * nq + i, h)),
            pl.BlockSpec((lp, HEAD_DIM), lambda b, h, i: (b, h)),
            pl.BlockSpec((lp, HEAD_DIM), lambda b, h, i: (b, h)),
            pl.BlockSpec((1, 1, lp), lambda b, h, i: (b * n_heads + h, 0, 0)),
        ],
        out_specs=pl.BlockSpec((tq, HEAD_DIM), lambda b, h, i: (b * nq + i, h)),
        out_shape=jax.ShapeDtypeStruct((m, n_heads * HEAD_DIM), BF16),
        scratch_shapes=[
            pltpu.VMEM((tq, 1), F32),
            pltpu.VMEM((tq, 1), F32),
            pltpu.VMEM((tq, HEAD_DIM), F32),
        ],
        compiler_params=_params("parallel", "parallel", "arbitrary"),
        name="forget_attn",
    )(q, k, v, c3)


def kernel(x, meta, a_norm, a_w_in, a_conv, a_w_out, kv_norm, w_kv, k_norm, w_f, b_f,
           b_norm, b_w_q, b_q_norm, b_w_o, ffn_norm, ffn_w_gu, ffn_w_down):
    batch, seq, d = x.shape
    lp = PAD + N_META + seq
    n_a = a_w_in.shape[0]
    n_b = b_w_q.shape[0]
    n_heads = w_f.shape[1]
    assert lp % BLOCK == 0 and (batch * lp) % ROW_TILE == 0

    meta_b = jnp.broadcast_to(meta.astype(x.dtype)[None], (batch, N_META, d))
    h = jnp.concatenate([jnp.zeros((batch, PAD, d), x.dtype), meta_b, x], axis=1)
    h = h.reshape(batch * lp, d)

    bf = lambda w: w.astype(BF16)
    k = v = c = None
    for layer in range(n_a + n_b):
        if layer < n_a:
            g = _conv_in(h, a_norm[layer], bf(a_w_in[layer]), a_conv[layer])
            h = _mm_res(g, bf(a_w_out[layer]), h, name="conv_out")
        else:
            if layer == n_a:
                k, v = _kv_proj(h, kv_norm, bf(w_kv), k_norm)
                lf = _log_forget(h, kv_norm, w_f, b_f, lp=lp)
                c = _cumsum_heads(lf, batch=batch, lp=lp, n_heads=n_heads)
            jb = layer - n_a
            q = _q_proj(h, b_norm[jb], bf(b_w_q[jb]), b_q_norm[jb])
            o = _attention(q, k, v, c, batch=batch, lp=lp, n_heads=n_heads)
            h = _mm_res(o, bf(b_w_o[jb]), h, name="attn_out")
        act = _ffn_up(h, ffn_norm[layer], bf(ffn_w_gu[layer]))
        h = _mm_res(act, bf(ffn_w_down[layer]), h, name="ffn_down")
    return h.reshape(batch, lp, d)[:, BLOCK:, :]
```

```python
import functools
import math

import jax
import jax.numpy as jnp
from jax import lax
from jax.experimental import pallas as pl
from jax.experimental.pallas import tpu as pltpu

N_META = 16
CONV_WIDTH = 3
HEAD_DIM = 128
BLOCK = 128
PAD = BLOCK - N_META
EPS = 1e-6
NEG = -1e30
LOG2E = 1.4426950408889634

F32 = jnp.float32
BF16 = jnp.bfloat16

ROW_TILE = 768
NORM_CHUNK = 32
VMEM_LIMIT_BYTES = 56 * 1024 * 1024


def _params(*semantics, flags=None):
    return pltpu.CompilerParams(dimension_semantics=semantics,
                                vmem_limit_bytes=VMEM_LIMIT_BYTES, flags=flags)


def _rms_rows(x_ref, g_ref, out_ref):
    rows = x_ref.shape[0]
    g = g_ref[...]

    def body(c, carry):
        r0 = pl.multiple_of(c * NORM_CHUNK, NORM_CHUNK)
        x = x_ref[pl.ds(r0, NORM_CHUNK), :]
        ms = jnp.mean(x * x, axis=-1, keepdims=True)
        out_ref[pl.ds(r0, NORM_CHUNK), :] = (x * lax.rsqrt(ms + EPS) * g).astype(out_ref.dtype)
        return carry

    lax.fori_loop(0, rows // NORM_CHUNK, body, 0)


def _head_rms(x, gain, scale):
    outs = []
    for hh in range(x.shape[1] // HEAD_DIM):
        xh = x[:, hh * HEAD_DIM:(hh + 1) * HEAD_DIM]
        ms = jnp.mean(xh * xh, axis=-1, keepdims=True)
        y = xh * lax.rsqrt(ms + EPS) * gain
        if scale is not None:
            y = y * scale
        outs.append(y)
    return outs


def _conv_in_kernel(h_ref, g_ref, wb_ref, wc_ref, wh_ref, cw_ref, o_ref,
                    xn_ref, halo_ref, u_ref):
    i = pl.program_id(0)
    j = pl.program_id(1)
    tm = o_ref.shape[0]

    @pl.when(j == 0)
    def _():
        _rms_rows(h_ref, g_ref, xn_ref)

    @pl.when(i == 0)
    def _():
        halo_ref[j] = jnp.zeros(halo_ref.shape[1:], F32)

    xn = xn_ref[...]
    bg = jnp.dot(xn, wb_ref[...], preferred_element_type=F32)
    cg = jnp.dot(xn, wc_ref[...], preferred_element_type=F32)
    hh = jnp.dot(xn, wh_ref[...], preferred_element_type=F32)
    u = cg * hh
    u_ref[0:8, :] = halo_ref[j]
    u_ref[8:, :] = u
    halo_ref[j] = u[tm - 8:, :]
    cw = cw_ref[...]
    conv = (u_ref[pl.ds(6, tm), :] * cw[0:1, :]
            + u_ref[pl.ds(7, tm), :] * cw[1:2, :]
            + u * cw[2:3, :])
    o_ref[...] = (bg * conv).astype(o_ref.dtype)


def _conv_in(h, gain, w_in, conv_w, *, tn=512):
    m, d = h.shape
    tm = ROW_TILE
    nj = d // tn
    return pl.pallas_call(
        _conv_in_kernel,
        grid=(m // tm, nj),
        in_specs=[
            pl.BlockSpec((tm, d), lambda i, j: (i, 0)),
            pl.BlockSpec((1, d), lambda i, j: (0, 0)),
            pl.BlockSpec((d, tn), lambda i, j: (0, j)),
            pl.BlockSpec((d, tn), lambda i, j: (0, nj + j)),
            pl.BlockSpec((d, tn), lambda i, j: (0, 2 * nj + j)),
            pl.BlockSpec((CONV_WIDTH, tn), lambda i, j: (0, j)),
        ],
        out_specs=pl.BlockSpec((tm, tn), lambda i, j: (i, j)),
        out_shape=jax.ShapeDtypeStruct((m, d), BF16),
        scratch_shapes=[
            pltpu.VMEM((tm, d), BF16),
            pltpu.VMEM((nj, 8, tn), F32),
            pltpu.VMEM((tm + 8, tn), F32),
        ],
        compiler_params=_params("arbitrary", "arbitrary"),
        name="conv_in",
    )(h, gain.reshape(1, d), w_in, w_in, w_in, conv_w)


def _mm_res_kernel(a_ref, w_ref, r_ref, o_ref):
    o_ref[...] = r_ref[...] + jnp.dot(a_ref[...], w_ref[...], preferred_element_type=F32)


def _mm_res(a, w, res, *, tn=1024, name="mm_res"):
    m, k = a.shape
    n = w.shape[1]
    tm = ROW_TILE
    tn = min(tn, n)
    return pl.pallas_call(
        _mm_res_kernel,
        grid=(m // tm, n // tn),
        in_specs=[
            pl.BlockSpec((tm, k), lambda i, j: (i, 0)),
            pl.BlockSpec((k, tn), lambda i, j: (0, j)),
            pl.BlockSpec((tm, tn), lambda i, j: (i, j)),
        ],
        out_specs=pl.BlockSpec((tm, tn), lambda i, j: (i, j)),
        out_shape=jax.ShapeDtypeStruct((m, n), F32),
        compiler_params=_params("parallel", "parallel"),
        name=name,
    )(a, w, res)


def _ffn_up_kernel(h_ref, g_ref, wg_ref, wu_ref, o_ref, xn_ref):
    @pl.when(pl.program_id(1) == 0)
    def _():
        _rms_rows(h_ref, g_ref, xn_ref)

    xn = xn_ref[...]
    g = jnp.dot(xn, wg_ref[...], preferred_element_type=F32)
    u = jnp.dot(xn, wu_ref[...], preferred_element_type=F32)
    o_ref[...] = (g * jax.nn.sigmoid(g) * u).astype(o_ref.dtype)


def _ffn_up(h, gain, w_gu, *, tn=512):
    m, d = h.shape
    f = w_gu.shape[1] // 2
    tm = ROW_TILE
    nj = f // tn
    return pl.pallas_call(
        _ffn_up_kernel,
        grid=(m // tm, nj),
        in_specs=[
            pl.BlockSpec((tm, d), lambda i, j: (i, 0)),
            pl.BlockSpec((1, d), lambda i, j: (0, 0)),
            pl.BlockSpec((d, tn), lambda i, j: (0, j)),
            pl.BlockSpec((d, tn), lambda i, j: (0, nj + j)),
        ],
        out_specs=pl.BlockSpec((tm, tn), lambda i, j: (i, j)),
        out_shape=jax.ShapeDtypeStruct((m, f), BF16),
        scratch_shapes=[pltpu.VMEM((tm, d), BF16)],
        compiler_params=_params("arbitrary", "arbitrary"),
        name="ffn_up",
    )(h, gain.reshape(1, d), w_gu, w_gu)


def _kv_kernel(h_ref, g_ref, wk_ref, wv_ref, kn_ref, k_ref, v_ref, xn_ref):
    @pl.when(pl.program_id(1) == 0)
    def _():
        _rms_rows(h_ref, g_ref, xn_ref)

    xn = xn_ref[...]
    k = jnp.dot(xn, wk_ref[...], preferred_element_type=F32)
    v = jnp.dot(xn, wv_ref[...], preferred_element_type=F32)
    for hh, kh in enumerate(_head_rms(k, kn_ref[...], None)):
        k_ref[:, hh * HEAD_DIM:(hh + 1) * HEAD_DIM] = kh.astype(k_ref.dtype)
    v_ref[...] = v.astype(v_ref.dtype)


def _kv_proj(h, gain, w_kv, k_norm, *, tn=512):
    m, d = h.shape
    hd = w_kv.shape[1] // 2
    tm = ROW_TILE
    nj = hd // tn
    return pl.pallas_call(
        _kv_kernel,
        grid=(m // tm, nj),
        in_specs=[
            pl.BlockSpec((tm, d), lambda i, j: (i, 0)),
            pl.BlockSpec((1, d), lambda i, j: (0, 0)),
            pl.BlockSpec((d, tn), lambda i, j: (0, j)),
            pl.BlockSpec((d, tn), lambda i, j: (0, nj + j)),
            pl.BlockSpec((1, HEAD_DIM), lambda i, j: (0, 0)),
        ],
        out_specs=[pl.BlockSpec((tm, tn), lambda i, j: (i, j)),
                   pl.BlockSpec((tm, tn), lambda i, j: (i, j))],
        out_shape=[jax.ShapeDtypeStruct((m, hd), BF16),
                   jax.ShapeDtypeStruct((m, hd), BF16)],
        scratch_shapes=[pltpu.VMEM((tm, d), BF16)],
        compiler_params=_params("arbitrary", "arbitrary"),
        name="kv_proj",
    )(h, gain.reshape(1, d), w_kv, w_kv, k_norm.reshape(1, HEAD_DIM))


def _q_kernel(h_ref, g_ref, wq_ref, qn_ref, q_ref, xn_ref):
    @pl.when(pl.program_id(1) == 0)
    def _():
        _rms_rows(h_ref, g_ref, xn_ref)

    q = jnp.dot(xn_ref[...], wq_ref[...], preferred_element_type=F32)
    scale = LOG2E / math.sqrt(HEAD_DIM)
    for hh, qh in enumerate(_head_rms(q, qn_ref[...], scale)):
        q_ref[:, hh * HEAD_DIM:(hh + 1) * HEAD_DIM] = qh.astype(q_ref.dtype)


def _q_proj(h, gain, w_q, q_norm, *, tn=1024):
    m, d = h.shape
    hd = w_q.shape[1]
    tm = ROW_TILE
    tn = min(tn, hd)
    return pl.pallas_call(
        _q_kernel,
        grid=(m // tm, hd // tn),
        in_specs=[
            pl.BlockSpec((tm, d), lambda i, j: (i, 0)),
            pl.BlockSpec((1, d), lambda i, j: (0, 0)),
            pl.BlockSpec((d, tn), lambda i, j: (0, j)),
            pl.BlockSpec((1, HEAD_DIM), lambda i, j: (0, 0)),
        ],
        out_specs=pl.BlockSpec((tm, tn), lambda i, j: (i, j)),
        out_shape=jax.ShapeDtypeStruct((m, hd), BF16),
        scratch_shapes=[pltpu.VMEM((tm, d), BF16)],
        compiler_params=_params("arbitrary", "arbitrary"),
        name="q_proj",
    )(h, gain.reshape(1, d), w_q, q_norm.reshape(1, HEAD_DIM))


def _logf_kernel(h_ref, g_ref, wf_ref, bf_ref, o_ref, *, lp):
    tm = h_ref.shape[0]
    x = h_ref[...]
    ms = jnp.mean(x * x, axis=-1, keepdims=True)
    xn = x * lax.rsqrt(ms + EPS) * g_ref[...]
    z = jnp.dot(xn, wf_ref[...], preferred_element_type=F32,
                precision=lax.Precision.HIGHEST) + bf_ref[...]
    lf = jnp.minimum(z, 0.0) - jnp.log1p(jnp.exp(-jnp.abs(z)))
    row = pl.program_id(0) * tm + lax.broadcasted_iota(jnp.int32, lf.shape, 0)
    o_ref[...] = jnp.where(row % lp >= PAD, lf, 0.0)


def _log_forget(h, gain, w_f, b_f, *, lp, tm=256):
    m, d = h.shape
    nh = w_f.shape[1]
    wf = jnp.zeros((d, HEAD_DIM), F32).at[:, :nh].set(w_f)
    bf = jnp.zeros((1, HEAD_DIM), F32).at[0, :nh].set(b_f)
    return pl.pallas_call(
        functools.partial(_logf_kernel, lp=lp),
        grid=(m // tm,),
        in_specs=[
            pl.BlockSpec((tm, d), lambda i: (i, 0)),
            pl.BlockSpec((1, d), lambda i: (0, 0)),
            pl.BlockSpec((d, HEAD_DIM), lambda i: (0, 0)),
            pl.BlockSpec((1, HEAD_DIM), lambda i: (0, 0)),
        ],
        out_specs=pl.BlockSpec((tm, HEAD_DIM), lambda i: (i, 0)),
        out_shape=jax.ShapeDtypeStruct((m, HEAD_DIM), F32),
        compiler_params=_params("parallel"),
        name="log_forget",
    )(h, gain.reshape(1, d), wf, bf)


def _key_bias_kernel(lf_ref, nb_ref, carry_ref):
    ch = lf_ref.shape[0]
    nh = nb_ref.shape[1]
    t = pl.program_id(1)

    @pl.when(t == 0)
    def _():
        carry_ref[...] = jnp.zeros(carry_ref.shape, F32)

    xt = lf_ref[...].T
    s_idx = lax.broadcasted_iota(jnp.int32, (ch, ch), 0)
    t_idx = lax.broadcasted_iota(jnp.int32, (ch, ch), 1)
    tri = (s_idx <= t_idx).astype(F32)
    ct = jnp.dot(xt, tri, preferred_element_type=F32,
                 precision=lax.Precision.HIGHEST) + carry_ref[:, 0:1]
    carry_ref[...] = jnp.broadcast_to(ct[:, ch - 1:ch], carry_ref.shape)
    pos = t * ch + lax.broadcasted_iota(jnp.int32, (nh, ch), 1)
    nb_ref[0] = jnp.where(pos >= PAD, -LOG2E * ct[:nh, :], NEG)


def _key_bias(lf, *, batch, lp, n_heads, ch=384):
    nch = lp // ch
    return pl.pallas_call(
        _key_bias_kernel,
        grid=(batch, nch),
        in_specs=[pl.BlockSpec((ch, HEAD_DIM), lambda b, t: (b * nch + t, 0))],
        out_specs=pl.BlockSpec((1, n_heads, ch), lambda b, t: (b, 0, t)),
        out_shape=jax.ShapeDtypeStruct((batch, n_heads, lp), F32),
        scratch_shapes=[pltpu.VMEM((HEAD_DIM, HEAD_DIM), F32)],
        compiler_params=_params("arbitrary", "arbitrary"),
        name="key_bias",
    )(lf)


def _attn_kernel(q_ref, k_ref, v_ref, nb_ref, o_ref,
                 qt_ref, nbc_ref, m_ref, l_ref, acc_ref):
    tq = q_ref.shape[0]
    tk = tq
    n_g = q_ref.shape[1] // HEAD_DIM
    lp = k_ref.shape[0]
    qi = pl.program_id(2)
    heads = [slice(g * HEAD_DIM, (g + 1) * HEAD_DIM) for g in range(n_g)]

    @pl.when(qi == 0)
    def _():
        for g in range(n_g):
            for ch in range(lp // BLOCK):
                row = nb_ref[0, g, ch:ch + 1, :]
                nbc_ref[g, ch * BLOCK:(ch + 1) * BLOCK, :] = (
                    jnp.broadcast_to(row, (BLOCK, BLOCK)).T)

    for g in range(n_g):
        qt_ref[g] = q_ref[:, heads[g]].astype(F32).T.astype(BF16)
    m_ref[...] = jnp.full(m_ref.shape, NEG, F32)
    l_ref[...] = jnp.zeros(l_ref.shape, F32)
    acc_ref[...] = jnp.zeros(acc_ref.shape, F32)

    def block(k0, diagonal):
        for g in range(n_g):
            st = jnp.dot(k_ref[pl.ds(k0, tk), heads[g]], qt_ref[g],
                         preferred_element_type=F32)
            nb = nbc_ref[g, pl.ds(k0, tk), :]
            st = st + jnp.concatenate([nb] * (tq // BLOCK), axis=1)
            if diagonal:
                kpos = lax.broadcasted_iota(jnp.int32, st.shape, 0)
                qpos = lax.broadcasted_iota(jnp.int32, st.shape, 1)
                st = jnp.where(kpos <= qpos, st, NEG)
            m_prev = m_ref[g]
            m_new = jnp.maximum(m_prev, jnp.max(st, axis=0, keepdims=True))
            p = jnp.exp2(st - m_new)
            alpha = jnp.exp2(m_prev - m_new)
            l_ref[g] = alpha * l_ref[g] + jnp.sum(p, axis=0, keepdims=True)
            pv = lax.dot_general(v_ref[pl.ds(k0, tk), heads[g]], p.astype(BF16),
                                 (((0,), (0,)), ((), ())),
                                 preferred_element_type=F32)
            acc_ref[g] = alpha * acc_ref[g] + pv
            m_ref[g] = m_new

    def body(kj, carry):
        block(pl.multiple_of(kj * tk, tk), False)
        return carry

    lax.fori_loop(0, qi, body, 0)
    block(pl.multiple_of(qi * tk, tk), True)
    for g in range(n_g):
        o_ref[:, heads[g]] = (acc_ref[g] / l_ref[g]).T.astype(o_ref.dtype)


def _attention(q, k, v, nb, *, batch, lp, n_heads, tq=384, n_g=4):
    m = q.shape[0]
    nq = lp // tq
    gw = n_g * HEAD_DIM
    n_hg = n_heads // n_g
    nb4 = nb.reshape(batch * n_hg, n_g, lp // BLOCK, BLOCK)
    return pl.pallas_call(
        _attn_kernel,
        grid=(batch, n_hg, nq),
        in_specs=[
            pl.BlockSpec((tq, gw), lambda b, h, i: (b * nq + i, h)),
            pl.BlockSpec((lp, gw), lambda b, h, i: (b, h)),
            pl.BlockSpec((lp, gw), lambda b, h, i: (b, h)),
            pl.BlockSpec((1, n_g, lp // BLOCK, BLOCK), lambda b, h, i: (b * n_hg + h, 0, 0, 0)),
        ],
        out_specs=pl.BlockSpec((tq, gw), lambda b, h, i: (b * nq + i, h)),
        out_shape=jax.ShapeDtypeStruct((m, n_heads * HEAD_DIM), BF16),
        scratch_shapes=[
            pltpu.VMEM((n_g, HEAD_DIM, tq), BF16),
            pltpu.VMEM((n_g, lp, BLOCK), F32),
            pltpu.VMEM((n_g, 1, tq), F32),
            pltpu.VMEM((n_g, 1, tq), F32),
            pltpu.VMEM((n_g, HEAD_DIM, tq), F32),
        ],
        compiler_params=_params("arbitrary", "arbitrary", "arbitrary"),
        name="forget_attn",
    )(q, k, v, nb4)


def kernel(x, meta, a_norm, a_w_in, a_conv, a_w_out, kv_norm, w_kv, k_norm, w_f, b_f,
           b_norm, b_w_q, b_q_norm, b_w_o, ffn_norm, ffn_w_gu, ffn_w_down):
    batch, seq, d = x.shape
    lp = PAD + N_META + seq
    n_a = a_w_in.shape[0]
    n_b = b_w_q.shape[0]
    n_heads = w_f.shape[1]
    assert lp % BLOCK == 0 and (batch * lp) % ROW_TILE == 0

    meta_b = jnp.broadcast_to(meta.astype(x.dtype)[None], (batch, N_META, d))
    h = jnp.concatenate([jnp.zeros((batch, PAD, d), x.dtype), meta_b, x], axis=1)
    h = h.reshape(batch * lp, d)

    bf = lambda w: w.astype(BF16)
    k = v = nb = None
    for layer in range(n_a + n_b):
        if layer < n_a:
            g = _conv_in(h, a_norm[layer], bf(a_w_in[layer]), a_conv[layer])
            h = _mm_res(g, bf(a_w_out[layer]), h, name="conv_out")
        else:
            if layer == n_a:
                k, v = _kv_proj(h, kv_norm, bf(w_kv), k_norm)
                lf = _log_forget(h, kv_norm, w_f, b_f, lp=lp)
                nb = _key_bias(lf, batch=batch, lp=lp, n_heads=n_heads)
            jb = layer - n_a
            q = _q_proj(h, b_norm[jb], bf(b_w_q[jb]), b_q_norm[jb])
            o = _attention(q, k, v, nb, batch=batch, lp=lp, n_heads=n_heads)
            h = _mm_res(o, bf(b_w_o[jb]), h, name="attn_out")
        act = _ffn_up(h, ffn_norm[layer], bf(ffn_w_gu[layer]))
        h = _mm_res(act, bf(ffn_w_down[layer]), h, name="ffn_down")
    return h.reshape(batch, lp, d)[:, BLOCK:, :]
```

```python
import functools
import math

import jax
import jax.numpy as jnp
from jax import lax
from jax.experimental import pallas as pl
from jax.experimental.pallas import tpu as pltpu

N_META = 16
CONV_WIDTH = 3
HEAD_DIM = 128
BLOCK = 128
PAD = BLOCK - N_META
EPS = 1e-6
NEG = -1e30
LOG2E = 1.4426950408889634

F32 = jnp.float32
BF16 = jnp.bfloat16

ROW_TILE = 768
RES_ROW_TILE = 512
COL_TILE = 512
VMEM_LIMIT_BYTES = 56 * 1024 * 1024


def _params(*semantics):
    return pltpu.CompilerParams(dimension_semantics=semantics,
                                vmem_limit_bytes=VMEM_LIMIT_BYTES)


def _unit_rms(h):
    ms = jnp.mean(h * h, axis=-1, keepdims=True)
    return h * lax.rsqrt(ms + EPS)


def _head_rms(x, gain, scale):
    outs = []
    for hh in range(x.shape[1] // HEAD_DIM):
        xh = x[:, hh * HEAD_DIM:(hh + 1) * HEAD_DIM]
        ms = jnp.mean(xh * xh, axis=-1, keepdims=True)
        y = xh * lax.rsqrt(ms + EPS) * gain
        if scale is not None:
            y = y * scale
        outs.append(y)
    return outs


def _fold_gain(w_refs, g_ref, w_sc):
    g = g_ref[...]
    for s, w_ref in enumerate(w_refs):
        w_sc[s] = (w_ref[...] * g).astype(BF16)


def _proj_call(kernel, xn, gain, w, layer, n_streams, n_out, extra_in, extra_specs,
               out_dtype, scratch, name, tn=COL_TILE, n_results=1):
    m, k = xn.shape
    tm = ROW_TILE
    tn = min(tn, n_out)
    nj = n_out // tn
    w_specs = [pl.BlockSpec((None, k, tn),
                            functools.partial(lambda j, i, s: (layer, 0, s * nj + j), s=s))
               for s in range(n_streams)]
    out_spec = pl.BlockSpec((tm, tn), lambda j, i: (i, j))
    out_shape = jax.ShapeDtypeStruct((m, n_out), out_dtype)
    return pl.pallas_call(
        kernel,
        grid=(nj, m // tm),
        in_specs=[pl.BlockSpec((tm, k), lambda j, i: (i, 0)),
                  pl.BlockSpec((k, 1), lambda j, i: (0, 0))] + w_specs + extra_specs,
        out_specs=out_spec if n_results == 1 else [out_spec] * n_results,
        out_shape=out_shape if n_results == 1 else [out_shape] * n_results,
        scratch_shapes=[pltpu.VMEM((n_streams, k, tn), BF16)] + scratch(tm, tn),
        compiler_params=_params("arbitrary", "arbitrary"),
        name=name,
    )(xn, gain.reshape(k, 1), *([w] * n_streams), *extra_in)


def _embed_kernel(x_ref, meta_ref, h_ref, xn_ref):
    t = pl.program_id(1)

    @pl.when(t == 0)
    def _():
        meta = meta_ref[...]
        h_ref[0:PAD, :] = jnp.zeros((PAD, h_ref.shape[1]), F32)
        h_ref[PAD:, :] = meta
        xn_ref[0:PAD, :] = jnp.zeros((PAD, h_ref.shape[1]), BF16)
        xn_ref[PAD:, :] = _unit_rms(meta).astype(BF16)

    @pl.when(t > 0)
    def _():
        rows = x_ref[0]
        h_ref[...] = rows
        xn_ref[...] = _unit_rms(rows).astype(BF16)


def _embed(x, meta):
    batch, seq, d = x.shape
    nb = (PAD + N_META + seq) // BLOCK
    shape = (batch * nb * BLOCK, d)
    return pl.pallas_call(
        _embed_kernel,
        grid=(batch, nb),
        in_specs=[
            pl.BlockSpec((1, BLOCK, d), lambda b, t: (b, jnp.maximum(t - 1, 0), 0)),
            pl.BlockSpec((N_META, d), lambda b, t: (0, 0)),
        ],
        out_specs=[pl.BlockSpec((BLOCK, d), lambda b, t: (b * nb + t, 0)),
                   pl.BlockSpec((BLOCK, d), lambda b, t: (b * nb + t, 0))],
        out_shape=[jax.ShapeDtypeStruct(shape, F32), jax.ShapeDtypeStruct(shape, BF16)],
        compiler_params=_params("arbitrary", "arbitrary"),
        name="embed",
    )(x, meta.astype(x.dtype))


def _conv_in_kernel(x_ref, g_ref, wb_ref, wc_ref, wh_ref, cw_ref, o_ref,
                    w_sc, halo_ref, u_ref):
    tm = o_ref.shape[0]

    @pl.when(pl.program_id(1) == 0)
    def _():
        _fold_gain((wb_ref, wc_ref, wh_ref), g_ref, w_sc)
        halo_ref[...] = jnp.zeros(halo_ref.shape, F32)

    x = x_ref[...]
    bg = jnp.dot(x, w_sc[0], preferred_element_type=F32)
    cg = jnp.dot(x, w_sc[1], preferred_element_type=F32)
    hh = jnp.dot(x, w_sc[2], preferred_element_type=F32)
    u = cg * hh
    u_ref[0:8, :] = halo_ref[...]
    u_ref[8:, :] = u
    halo_ref[...] = u[tm - 8:, :]
    cw = cw_ref[...]
    conv = (u_ref[pl.ds(6, tm), :] * cw[0:1, :]
            + u_ref[pl.ds(7, tm), :] * cw[1:2, :]
            + u * cw[2:3, :])
    o_ref[...] = (bg * conv).astype(o_ref.dtype)


def _conv_in(xn, gain, w_in, layer, conv_w):
    d = xn.shape[1]
    tn = min(COL_TILE, d)
    return _proj_call(
        _conv_in_kernel, xn, gain, w_in, layer, 3, d,
        [conv_w], [pl.BlockSpec((CONV_WIDTH, tn), lambda j, i: (0, j))], BF16,
        lambda tm, tn: [pltpu.VMEM((8, tn), F32), pltpu.VMEM((tm + 8, tn), F32)],
        "conv_in")


def _ffn_up_kernel(x_ref, g_ref, wg_ref, wu_ref, o_ref, w_sc):
    @pl.when(pl.program_id(1) == 0)
    def _():
        _fold_gain((wg_ref, wu_ref), g_ref, w_sc)

    x = x_ref[...]
    g = jnp.dot(x, w_sc[0], preferred_element_type=F32)
    u = jnp.dot(x, w_sc[1], preferred_element_type=F32)
    o_ref[...] = (g * jax.nn.sigmoid(g) * u).astype(o_ref.dtype)


def _ffn_up(xn, gain, w_gu, layer):
    return _proj_call(_ffn_up_kernel, xn, gain, w_gu, layer, 2, w_gu.shape[2] // 2,
                      [], [], BF16, lambda tm, tn: [], "ffn_up")


def _kv_kernel(x_ref, g_ref, wk_ref, wv_ref, kn_ref, k_ref, v_ref, w_sc):
    @pl.when(pl.program_id(1) == 0)
    def _():
        _fold_gain((wk_ref, wv_ref), g_ref, w_sc)

    x = x_ref[...]
    k = jnp.dot(x, w_sc[0], preferred_element_type=F32)
    v = jnp.dot(x, w_sc[1], preferred_element_type=F32)
    for hh, kh in enumerate(_head_rms(k, kn_ref[...], None)):
        k_ref[:, hh * HEAD_DIM:(hh + 1) * HEAD_DIM] = kh.astype(k_ref.dtype)
    v_ref[...] = v.astype(v_ref.dtype)


def _kv_proj(xn, gain, w_kv, k_norm):
    return _proj_call(
        _kv_kernel, xn, gain, w_kv[None], 0, 2, w_kv.shape[1] // 2,
        [k_norm.reshape(1, HEAD_DIM)], [pl.BlockSpec((1, HEAD_DIM), lambda j, i: (0, 0))],
        BF16, lambda tm, tn: [], "kv_proj", n_results=2)


def _q_kernel(x_ref, g_ref, wq_ref, qn_ref, q_ref, w_sc):
    @pl.when(pl.program_id(1) == 0)
    def _():
        _fold_gain((wq_ref,), g_ref, w_sc)

    q = jnp.dot(x_ref[...], w_sc[0], preferred_element_type=F32)
    scale = LOG2E / math.sqrt(HEAD_DIM)
    for hh, qh in enumerate(_head_rms(q, qn_ref[...], scale)):
        q_ref[:, hh * HEAD_DIM:(hh + 1) * HEAD_DIM] = qh.astype(q_ref.dtype)


def _q_proj(xn, gain, w_q, layer, q_norm):
    return _proj_call(
        _q_kernel, xn, gain, w_q, layer, 1, w_q.shape[2],
        [q_norm.reshape(1, HEAD_DIM)], [pl.BlockSpec((1, HEAD_DIM), lambda j, i: (0, 0))],
        BF16, lambda tm, tn: [], "q_proj", tn=2 * COL_TILE)


def _res_norm_kernel(a_ref, w_ref, r_ref, h_ref, xn_ref, *, nk):
    k = pl.program_id(1)
    part = jnp.dot(a_ref[...], w_ref[...], preferred_element_type=F32)

    def finish(h):
        h_ref[...] = h
        xn_ref[...] = _unit_rms(h).astype(xn_ref.dtype)

    if nk == 1:
        finish(r_ref[...] + part)
        return

    @pl.when(k == 0)
    def _():
        h_ref[...] = r_ref[...] + part

    @pl.when((k > 0) & (k < nk - 1))
    def _():
        h_ref[...] = h_ref[...] + part

    @pl.when(k == nk - 1)
    def _():
        finish(h_ref[...] + part)


def _res_norm(a, w, layer, res, *, name):
    m, kk = a.shape
    n = w.shape[2]
    tm = RES_ROW_TILE
    nk = 1 if kk * n * 2 <= 10 * 1024 * 1024 else 2
    tk = kk // nk
    return pl.pallas_call(
        functools.partial(_res_norm_kernel, nk=nk),
        grid=(m // tm, nk),
        in_specs=[
            pl.BlockSpec((tm, tk), lambda i, k: (i, k)),
            pl.BlockSpec((None, tk, n), lambda i, k: (layer, k, 0)),
            pl.BlockSpec((tm, n), lambda i, k: (i, 0)),
        ],
        out_specs=[pl.BlockSpec((tm, n), lambda i, k: (i, 0)),
                   pl.BlockSpec((tm, n), lambda i, k: (i, 0))],
        out_shape=[jax.ShapeDtypeStruct((m, n), F32), jax.ShapeDtypeStruct((m, n), BF16)],
        compiler_params=_params("arbitrary", "arbitrary"),
        name=name,
    )(a, w, res)


def _logf_kernel(h_ref, g_ref, wf_ref, bf_ref, o_ref, *, lp):
    tm = h_ref.shape[0]
    xn = _unit_rms(h_ref[...]) * g_ref[...]
    z = jnp.dot(xn, wf_ref[...], preferred_element_type=F32,
                precision=lax.Precision.HIGHEST) + bf_ref[...]
    lf = jnp.minimum(z, 0.0) - jnp.log1p(jnp.exp(-jnp.abs(z)))
    row = pl.program_id(0) * tm + lax.broadcasted_iota(jnp.int32, lf.shape, 0)
    o_ref[...] = jnp.where(row % lp >= PAD, lf, 0.0)


def _log_forget(h, gain, w_f, b_f, *, lp, tm=256):
    m, d = h.shape
    nh = w_f.shape[1]
    wf = jnp.zeros((d, HEAD_DIM), F32).at[:, :nh].set(w_f)
    bf = jnp.zeros((1, HEAD_DIM), F32).at[0, :nh].set(b_f)
    return pl.pallas_call(
        functools.partial(_logf_kernel, lp=lp),
        grid=(m // tm,),
        in_specs=[
            pl.BlockSpec((tm, d), lambda i: (i, 0)),
            pl.BlockSpec((1, d), lambda i: (0, 0)),
            pl.BlockSpec((d, HEAD_DIM), lambda i: (0, 0)),
            pl.BlockSpec((1, HEAD_DIM), lambda i: (0, 0)),
        ],
        out_specs=pl.BlockSpec((tm, HEAD_DIM), lambda i: (i, 0)),
        out_shape=jax.ShapeDtypeStruct((m, HEAD_DIM), F32),
        compiler_params=_params("parallel"),
        name="log_forget",
    )(h, gain.reshape(1, d), wf, bf)


def _key_bias_kernel(lf_ref, nb_ref, carry_ref):
    ch = lf_ref.shape[0]
    nh = nb_ref.shape[1]
    t = pl.program_id(1)

    @pl.when(t == 0)
    def _():
        carry_ref[...] = jnp.zeros(carry_ref.shape, F32)

    xt = lf_ref[...].T
    s_idx = lax.broadcasted_iota(jnp.int32, (ch, ch), 0)
    t_idx = lax.broadcasted_iota(jnp.int32, (ch, ch), 1)
    tri = (s_idx <= t_idx).astype(F32)
    ct = jnp.dot(xt, tri, preferred_element_type=F32,
                 precision=lax.Precision.HIGHEST) + carry_ref[:, 0:1]
    carry_ref[...] = jnp.broadcast_to(ct[:, ch - 1:ch], carry_ref.shape)
    pos = t * ch + lax.broadcasted_iota(jnp.int32, (nh, ch), 1)
    nb_ref[0] = jnp.where(pos >= PAD, -LOG2E * ct[:nh, :], NEG)


def _key_bias(lf, *, batch, lp, n_heads, ch=384):
    nch = lp // ch
    return pl.pallas_call(
        _key_bias_kernel,
        grid=(batch, nch),
        in_specs=[pl.BlockSpec((ch, HEAD_DIM), lambda b, t: (b * nch + t, 0))],
        out_specs=pl.BlockSpec((1, n_heads, ch), lambda b, t: (b, 0, t)),
        out_shape=jax.ShapeDtypeStruct((batch, n_heads, lp), F32),
        scratch_shapes=[pltpu.VMEM((HEAD_DIM, HEAD_DIM), F32)],
        compiler_params=_params("arbitrary", "arbitrary"),
        name="key_bias",
    )(lf)


def _attn_kernel(q_ref, k_ref, v_ref, nb_ref, o_ref,
                 qt_ref, nbc_ref, m_ref, l_ref, acc_ref):
    tq = q_ref.shape[0]
    tk = tq
    n_g = q_ref.shape[1] // HEAD_DIM
    lp = k_ref.shape[0]
    qi = pl.program_id(2)
    heads = [slice(g * HEAD_DIM, (g + 1) * HEAD_DIM) for g in range(n_g)]

    @pl.when(qi == 0)
    def _():
        for g in range(n_g):
            for ch in range(lp // BLOCK):
                row = nb_ref[0, g, ch:ch + 1, :]
                nbc_ref[g, ch * BLOCK:(ch + 1) * BLOCK, :] = (
                    jnp.broadcast_to(row, (BLOCK, BLOCK)).T)

    for g in range(n_g):
        qt_ref[g] = q_ref[:, heads[g]].astype(F32).T.astype(BF16)
    m_ref[...] = jnp.full(m_ref.shape, NEG, F32)
    l_ref[...] = jnp.zeros(l_ref.shape, F32)
    acc_ref[...] = jnp.zeros(acc_ref.shape, F32)

    def block(k0, diagonal):
        for g in range(n_g):
            st = jnp.dot(k_ref[pl.ds(k0, tk), heads[g]], qt_ref[g],
                         preferred_element_type=F32)
            nb = nbc_ref[g, pl.ds(k0, tk), :]
            st = st + jnp.concatenate([nb] * (tq // BLOCK), axis=1)
            if diagonal:
                kpos = lax.broadcasted_iota(jnp.int32, st.shape, 0)
                qpos = lax.broadcasted_iota(jnp.int32, st.shape, 1)
                st = jnp.where(kpos <= qpos, st, NEG)
            m_prev = m_ref[g]
            m_new = jnp.maximum(m_prev, jnp.max(st, axis=0, keepdims=True))
            p = jnp.exp2(st - m_new)
            alpha = jnp.exp2(m_prev - m_new)
            l_ref[g] = alpha * l_ref[g] + jnp.sum(p, axis=0, keepdims=True)
            pv = lax.dot_general(v_ref[pl.ds(k0, tk), heads[g]], p.astype(BF16),
                                 (((0,), (0,)), ((), ())),
                                 preferred_element_type=F32)
            acc_ref[g] = alpha * acc_ref[g] + pv
            m_ref[g] = m_new

    def body(kj, carry):
        block(pl.multiple_of(kj * tk, tk), False)
        return carry

    lax.fori_loop(0, qi, body, 0)
    block(pl.multiple_of(qi * tk, tk), True)
    for g in range(n_g):
        o_ref[:, heads[g]] = (acc_ref[g] / l_ref[g]).T.astype(o_ref.dtype)


def _attention(q, k, v, nb, *, batch, lp, n_heads, tq=384, n_g=4):
    m = q.shape[0]
    nq = lp // tq
    gw = n_g * HEAD_DIM
    n_hg = n_heads // n_g
    nb4 = nb.reshape(batch * n_hg, n_g, lp // BLOCK, BLOCK)
    return pl.pallas_call(
        _attn_kernel,
        grid=(batch, n_hg, nq),
        in_specs=[
            pl.BlockSpec((tq, gw), lambda b, h, i: (b * nq + i, h)),
            pl.BlockSpec((lp, gw), lambda b, h, i: (b, h)),
            pl.BlockSpec((lp, gw), lambda b, h, i: (b, h)),
            pl.BlockSpec((1, n_g, lp // BLOCK, BLOCK), lambda b, h, i: (b * n_hg + h, 0, 0, 0)),
        ],
        out_specs=pl.BlockSpec((tq, gw), lambda b, h, i: (b * nq + i, h)),
        out_shape=jax.ShapeDtypeStruct((m, n_heads * HEAD_DIM), BF16),
        scratch_shapes=[
            pltpu.VMEM((n_g, HEAD_DIM, tq), BF16),
            pltpu.VMEM((n_g, lp, BLOCK), F32),
            pltpu.VMEM((n_g, 1, tq), F32),
            pltpu.VMEM((n_g, 1, tq), F32),
            pltpu.VMEM((n_g, HEAD_DIM, tq), F32),
        ],
        compiler_params=_params("arbitrary", "arbitrary", "arbitrary"),
        name="forget_attn",
    )(q, k, v, nb4)


def kernel(x, meta, a_norm, a_w_in, a_conv, a_w_out, kv_norm, w_kv, k_norm, w_f, b_f,
           b_norm, b_w_q, b_q_norm, b_w_o, ffn_norm, ffn_w_gu, ffn_w_down):
    batch, seq, d = x.shape
    lp = PAD + N_META + seq
    n_a = a_w_in.shape[0]
    n_b = b_w_q.shape[0]
    n_heads = w_f.shape[1]
    assert lp % BLOCK == 0 and seq % BLOCK == 0
    assert (batch * lp) % ROW_TILE == 0 and (batch * lp) % RES_ROW_TILE == 0

    h, xn = _embed(x, meta)
    w_out_bf = a_w_out.astype(BF16)
    w_o_bf = b_w_o.astype(BF16)
    w_down_bf = ffn_w_down.astype(BF16)
    k = v = nb = None
    for layer in range(n_a + n_b):
        if layer < n_a:
            g = _conv_in(xn, a_norm[layer], a_w_in, layer, a_conv[layer])
            h, xn = _res_norm(g, w_out_bf, layer, h, name="conv_out")
        else:
            if layer == n_a:
                k, v = _kv_proj(xn, kv_norm, w_kv, k_norm)
                lf = _log_forget(h, kv_norm, w_f, b_f, lp=lp)
                nb = _key_bias(lf, batch=batch, lp=lp, n_heads=n_heads)
            jb = layer - n_a
            q = _q_proj(xn, b_norm[jb], b_w_q, jb, b_q_norm[jb])
            o = _attention(q, k, v, nb, batch=batch, lp=lp, n_heads=n_heads)
            h, xn = _res_norm(o, w_o_bf, jb, h, name="attn_out")
        act = _ffn_up(xn, ffn_norm[layer], ffn_w_gu, layer)
        h, xn = _res_norm(act, w_down_bf, layer, h, name="ffn_down")
    return h.reshape(batch, lp, d)[:, BLOCK:, :]
```

```python
import functools
import math

import jax
import jax.numpy as jnp
from jax import lax
from jax.experimental import pallas as pl
from jax.experimental.pallas import tpu as pltpu

N_META = 16
CONV_WIDTH = 3
HEAD_DIM = 128
BLOCK = 128
PAD = BLOCK - N_META
EPS = 1e-6
NEG = -1e30
LOG2E = 1.4426950408889634

F32 = jnp.float32
BF16 = jnp.bfloat16

ROW_TILE = 768
RES_ROW_TILE = 512
COL_TILE = 512
VMEM_LIMIT_BYTES = 56 * 1024 * 1024


def _params(*semantics):
    return pltpu.CompilerParams(dimension_semantics=semantics,
                                vmem_limit_bytes=VMEM_LIMIT_BYTES)


def _unit_rms(h):
    ms = jnp.mean(h * h, axis=-1, keepdims=True)
    return h * lax.rsqrt(ms + EPS)


def _head_rms(x, gain, scale):
    outs = []
    for hh in range(x.shape[1] // HEAD_DIM):
        xh = x[:, hh * HEAD_DIM:(hh + 1) * HEAD_DIM]
        ms = jnp.mean(xh * xh, axis=-1, keepdims=True)
        y = xh * lax.rsqrt(ms + EPS) * gain
        if scale is not None:
            y = y * scale
        outs.append(y)
    return outs


def _fold_gain(w_refs, g_ref, w_sc):
    g = g_ref[...]
    for s, w_ref in enumerate(w_refs):
        w_sc[s] = (w_ref[...] * g).astype(BF16)


def _proj_call(kernel, xn, gain, w, layer, n_streams, n_out, extra_in, extra_specs,
               out_dtype, scratch, name, tm=ROW_TILE, tn=COL_TILE, n_results=1):
    m, k = xn.shape
    tn = min(tn, n_out)
    nj = n_out // tn
    w_specs = [pl.BlockSpec((None, k, tn),
                            functools.partial(lambda j, i, s: (layer, 0, s * nj + j), s=s))
               for s in range(n_streams)]
    out_spec = pl.BlockSpec((tm, tn), lambda j, i: (i, j))
    out_shape = jax.ShapeDtypeStruct((m, n_out), out_dtype)
    return pl.pallas_call(
        kernel,
        grid=(nj, m // tm),
        in_specs=[pl.BlockSpec((tm, k), lambda j, i: (i, 0)),
                  pl.BlockSpec((k, 1), lambda j, i: (0, 0))] + w_specs + extra_specs,
        out_specs=out_spec if n_results == 1 else [out_spec] * n_results,
        out_shape=out_shape if n_results == 1 else [out_shape] * n_results,
        scratch_shapes=[pltpu.VMEM((n_streams, k, tn), BF16)] + scratch(tm, tn),
        compiler_params=_params("arbitrary", "arbitrary"),
        name=name,
    )(xn, gain.reshape(k, 1), *([w] * n_streams), *extra_in)


def _embed_kernel(x_ref, meta_ref, h_ref, xn_ref):
    t = pl.program_id(1)

    rows = h_ref.shape[0]

    @pl.when(t == 0)
    def _():
        meta = meta_ref[...]
        head = x_ref[0:rows - BLOCK, :]
        h_ref[0:PAD, :] = jnp.zeros((PAD, h_ref.shape[1]), F32)
        h_ref[PAD:BLOCK, :] = meta
        h_ref[BLOCK:, :] = head
        xn_ref[0:PAD, :] = jnp.zeros((PAD, h_ref.shape[1]), BF16)
        xn_ref[PAD:BLOCK, :] = _unit_rms(meta).astype(BF16)
        xn_ref[BLOCK:, :] = _unit_rms(head).astype(BF16)

    @pl.when(t > 0)
    def _():
        body = x_ref[...]
        h_ref[...] = body
        xn_ref[...] = _unit_rms(body).astype(BF16)


def _embed(x, meta, *, rows=3 * BLOCK):
    batch, seq, d = x.shape
    lp = PAD + N_META + seq
    nb = lp // rows
    shape = (batch * lp, d)
    x_row0 = lambda t: pl.multiple_of(jnp.maximum(t * rows - BLOCK, 0), BLOCK)
    return pl.pallas_call(
        _embed_kernel,
        grid=(batch, nb),
        in_specs=[
            pl.BlockSpec((None, pl.Element(rows), pl.Element(d)),
                         lambda b, t: (b, x_row0(t), 0)),
            pl.BlockSpec((N_META, d), lambda b, t: (0, 0)),
        ],
        out_specs=[pl.BlockSpec((rows, d), lambda b, t: (b * nb + t, 0)),
                   pl.BlockSpec((rows, d), lambda b, t: (b * nb + t, 0))],
        out_shape=[jax.ShapeDtypeStruct(shape, F32), jax.ShapeDtypeStruct(shape, BF16)],
        compiler_params=_params("arbitrary", "arbitrary"),
        name="embed",
    )(x, meta.astype(x.dtype))


def _conv_in_kernel(x_ref, g_ref, wb_ref, wc_ref, wh_ref, cw_ref, o_ref,
                    w_sc, halo_ref, u_ref):
    tm = o_ref.shape[0]

    @pl.when(pl.program_id(1) == 0)
    def _():
        _fold_gain((wb_ref, wc_ref, wh_ref), g_ref, w_sc)
        halo_ref[...] = jnp.zeros(halo_ref.shape, F32)

    x = x_ref[...]
    bg = jnp.dot(x, w_sc[0], preferred_element_type=F32)
    cg = jnp.dot(x, w_sc[1], preferred_element_type=F32)
    hh = jnp.dot(x, w_sc[2], preferred_element_type=F32)
    u = cg * hh
    u_ref[0:8, :] = halo_ref[...]
    u_ref[8:, :] = u
    halo_ref[...] = u[tm - 8:, :]
    cw = cw_ref[...]
    conv = (u_ref[pl.ds(6, tm), :] * cw[0:1, :]
            + u_ref[pl.ds(7, tm), :] * cw[1:2, :]
            + u * cw[2:3, :])
    o_ref[...] = (bg * conv).astype(o_ref.dtype)


def _conv_in(xn, gain, w_in, layer, conv_w):
    d = xn.shape[1]
    tn = min(COL_TILE, d)
    return _proj_call(
        _conv_in_kernel, xn, gain, w_in, layer, 3, d,
        [conv_w], [pl.BlockSpec((CONV_WIDTH, tn), lambda j, i: (0, j))], BF16,
        lambda tm, tn: [pltpu.VMEM((8, tn), F32), pltpu.VMEM((tm + 8, tn), F32)],
        "conv_in")


def _ffn_up_kernel(x_ref, g_ref, wg_ref, wu_ref, o_ref, w_sc):
    @pl.when(pl.program_id(1) == 0)
    def _():
        _fold_gain((wg_ref, wu_ref), g_ref, w_sc)

    x = x_ref[...]
    g = jnp.dot(x, w_sc[0], preferred_element_type=F32)
    u = jnp.dot(x, w_sc[1], preferred_element_type=F32)
    o_ref[...] = (g * jax.nn.sigmoid(g) * u).astype(o_ref.dtype)


def _ffn_up(xn, gain, w_gu, layer):
    return _proj_call(_ffn_up_kernel, xn, gain, w_gu, layer, 2, w_gu.shape[2] // 2,
                      [], [], BF16, lambda tm, tn: [], "ffn_up", tm=2 * ROW_TILE)


def _kv_kernel(x_ref, g_ref, wk_ref, wv_ref, kn_ref, k_ref, v_ref, w_sc):
    @pl.when(pl.program_id(1) == 0)
    def _():
        _fold_gain((wk_ref, wv_ref), g_ref, w_sc)

    x = x_ref[...]
    k = jnp.dot(x, w_sc[0], preferred_element_type=F32)
    v = jnp.dot(x, w_sc[1], preferred_element_type=F32)
    for hh, kh in enumerate(_head_rms(k, kn_ref[...], None)):
        k_ref[:, hh * HEAD_DIM:(hh + 1) * HEAD_DIM] = kh.astype(k_ref.dtype)
    v_ref[...] = v.astype(v_ref.dtype)


def _kv_proj(xn, gain, w_kv, k_norm):
    return _proj_call(
        _kv_kernel, xn, gain, w_kv[None], 0, 2, w_kv.shape[1] // 2,
        [k_norm.reshape(1, HEAD_DIM)], [pl.BlockSpec((1, HEAD_DIM), lambda j, i: (0, 0))],
        BF16, lambda tm, tn: [], "kv_proj", n_results=2)


def _q_kernel(x_ref, g_ref, wq_ref, qn_ref, q_ref, w_sc):
    @pl.when(pl.program_id(1) == 0)
    def _():
        _fold_gain((wq_ref,), g_ref, w_sc)

    q = jnp.dot(x_ref[...], w_sc[0], preferred_element_type=F32)
    scale = LOG2E / math.sqrt(HEAD_DIM)
    for hh, qh in enumerate(_head_rms(q, qn_ref[...], scale)):
        q_ref[:, hh * HEAD_DIM:(hh + 1) * HEAD_DIM] = qh.astype(q_ref.dtype)


def _q_proj(xn, gain, w_q, layer, q_norm):
    return _proj_call(
        _q_kernel, xn, gain, w_q, layer, 1, w_q.shape[2],
        [q_norm.reshape(1, HEAD_DIM)], [pl.BlockSpec((1, HEAD_DIM), lambda j, i: (0, 0))],
        BF16, lambda tm, tn: [], "q_proj", tn=2 * COL_TILE)


def _res_kernel(a_ref, w_ref, r_ref, h_ref, *xn_ref):
    h = r_ref[...] + jnp.dot(a_ref[...], w_ref[...], preferred_element_type=F32)
    h_ref[...] = h
    if xn_ref:
        xn_ref[0][...] = _unit_rms(h).astype(xn_ref[0].dtype)


def _res_call(a, w, layer, res, *, slab, n_slabs, emit_xn, name, unpad=None):
    kk = a.shape[1]
    n = w.shape[2]
    tm = RES_ROW_TILE
    tk = kk // n_slabs
    w_spec_kwargs = dict(pipeline_mode=pl.Buffered(1))
    if unpad is None:
        m = res.shape[0]
        grid = (m // tm,)
        a_spec = pl.BlockSpec((tm, tk), lambda i: (i, slab))
        w_spec = pl.BlockSpec((None, tk, n), lambda i: (layer, slab, 0), **w_spec_kwargs)
        r_spec = pl.BlockSpec((tm, n), lambda i: (i, 0))
        o_spec = pl.BlockSpec((tm, n), lambda i: (i, 0))
        semantics = ("arbitrary",)
    else:
        batch, lp, seq = unpad
        m = batch * seq
        nt = seq // tm
        grid = (batch, nt)
        row0 = lambda b, t: pl.multiple_of(b * lp + BLOCK + t * tm, BLOCK)
        a_spec = pl.BlockSpec((pl.Element(tm), pl.Element(tk)),
                              lambda b, t: (row0(b, t), slab * tk))
        w_spec = pl.BlockSpec((None, tk, n), lambda b, t: (layer, slab, 0), **w_spec_kwargs)
        r_spec = pl.BlockSpec((pl.Element(tm), pl.Element(n)),
                              lambda b, t: (row0(b, t), 0))
        o_spec = pl.BlockSpec((tm, n), lambda b, t: (b * nt + t, 0))
        semantics = ("arbitrary", "arbitrary")
    out_specs = [o_spec, o_spec] if emit_xn else o_spec
    out_shape = [jax.ShapeDtypeStruct((m, n), F32), jax.ShapeDtypeStruct((m, n), BF16)]
    return pl.pallas_call(
        _res_kernel,
        grid=grid,
        in_specs=[a_spec, w_spec, r_spec],
        out_specs=out_specs,
        out_shape=out_shape if emit_xn else out_shape[0],
        compiler_params=_params(*semantics),
        name=name,
    )(a, w, res)


def _res_norm(a, w, layer, res, *, name, unpad=None):
    kk = a.shape[1]
    n = w.shape[2]
    n_slabs = 1 if kk * n * 2 <= 12 * 1024 * 1024 else 2
    h, xn = res, None
    for slab in range(n_slabs):
        last = slab == n_slabs - 1
        out = _res_call(a, w, layer, h, slab=slab, n_slabs=n_slabs,
                        emit_xn=last and unpad is None, name=name,
                        unpad=unpad if last else None)
        h, xn = out if last and unpad is None else (out, None)
    return h, xn


def _logf_kernel(h_ref, g_ref, wf_ref, bf_ref, o_ref, *, lp):
    tm = h_ref.shape[0]
    xn = _unit_rms(h_ref[...]) * g_ref[...]
    z = jnp.dot(xn, wf_ref[...], preferred_element_type=F32,
                precision=lax.Precision.HIGHEST) + bf_ref[...]
    lf = jnp.minimum(z, 0.0) - jnp.log1p(jnp.exp(-jnp.abs(z)))
    row = pl.program_id(0) * tm + lax.broadcasted_iota(jnp.int32, lf.shape, 0)
    o_ref[...] = jnp.where(row % lp >= PAD, lf, 0.0)


def _log_forget(h, gain, w_f, b_f, *, lp, tm=256):
    m, d = h.shape
    nh = w_f.shape[1]
    wf = jnp.zeros((d, HEAD_DIM), F32).at[:, :nh].set(w_f)
    bf = jnp.zeros((1, HEAD_DIM), F32).at[0, :nh].set(b_f)
    return pl.pallas_call(
        functools.partial(_logf_kernel, lp=lp),
        grid=(m // tm,),
        in_specs=[
            pl.BlockSpec((tm, d), lambda i: (i, 0)),
            pl.BlockSpec((1, d), lambda i: (0, 0)),
            pl.BlockSpec((d, HEAD_DIM), lambda i: (0, 0)),
            pl.BlockSpec((1, HEAD_DIM), lambda i: (0, 0)),
        ],
        out_specs=pl.BlockSpec((tm, HEAD_DIM), lambda i: (i, 0)),
        out_shape=jax.ShapeDtypeStruct((m, HEAD_DIM), F32),
        compiler_params=_params("parallel"),
        name="log_forget",
    )(h, gain.reshape(1, d), wf, bf)


def _key_bias_kernel(lf_ref, nb_ref, carry_ref):
    ch = lf_ref.shape[0]
    nh = nb_ref.shape[1]
    t = pl.program_id(1)

    @pl.when(t == 0)
    def _():
        carry_ref[...] = jnp.zeros(carry_ref.shape, F32)

    xt = lf_ref[...].T
    s_idx = lax.broadcasted_iota(jnp.int32, (ch, ch), 0)
    t_idx = lax.broadcasted_iota(jnp.int32, (ch, ch), 1)
    tri = (s_idx <= t_idx).astype(F32)
    ct = jnp.dot(xt, tri, preferred_element_type=F32,
                 precision=lax.Precision.HIGHEST) + carry_ref[:, 0:1]
    carry_ref[...] = jnp.broadcast_to(ct[:, ch - 1:ch], carry_ref.shape)
    pos = t * ch + lax.broadcasted_iota(jnp.int32, (nh, ch), 1)
    nb_ref[0] = jnp.where(pos >= PAD, -LOG2E * ct[:nh, :], NEG)


def _key_bias(lf, *, batch, lp, n_heads, ch=384):
    nch = lp // ch
    return pl.pallas_call(
        _key_bias_kernel,
        grid=(batch, nch),
        in_specs=[pl.BlockSpec((ch, HEAD_DIM), lambda b, t: (b * nch + t, 0))],
        out_specs=pl.BlockSpec((1, n_heads, ch), lambda b, t: (b, 0, t)),
        out_shape=jax.ShapeDtypeStruct((batch, n_heads, lp), F32),
        scratch_shapes=[pltpu.VMEM((HEAD_DIM, HEAD_DIM), F32)],
        compiler_params=_params("arbitrary", "arbitrary"),
        name="key_bias",
    )(lf)


def _attn_kernel(q_ref, k_ref, v_ref, nb_ref, o_ref,
                 qt_ref, nbc_ref, m_ref, l_ref, acc_ref, s_ref):
    tq = q_ref.shape[0]
    n_g = q_ref.shape[1] // HEAD_DIM
    lp = k_ref.shape[0]
    qi = pl.program_id(2)
    heads = [slice(g * HEAD_DIM, (g + 1) * HEAD_DIM) for g in range(n_g)]

    @pl.when(qi == 0)
    def _():
        for g in range(n_g):
            for ch in range(lp // BLOCK):
                row = nb_ref[0, g, ch:ch + 1, :]
                nbc_ref[g, ch * BLOCK:(ch + 1) * BLOCK, :] = (
                    jnp.broadcast_to(row, (BLOCK, BLOCK)).T)

    for g in range(n_g):
        qt_ref[g] = q_ref[:, heads[g]].astype(F32).T.astype(BF16)
    m_ref[...] = jnp.full(m_ref.shape, NEG, F32)
    l_ref[...] = jnp.zeros(l_ref.shape, F32)
    acc_ref[...] = jnp.zeros(acc_ref.shape, F32)

    def block(k0, tk, diagonal):
        for g in range(n_g):
            s_ref[g, 0:tk, :] = jnp.dot(k_ref[pl.ds(k0, tk), heads[g]], qt_ref[g],
                                        preferred_element_type=F32)
        for g in range(n_g):
            nb = nbc_ref[g, pl.ds(k0, tk), :]
            st = s_ref[g, 0:tk, :] + jnp.concatenate([nb] * (tq // BLOCK), axis=1)
            if diagonal:
                kpos = lax.broadcasted_iota(jnp.int32, st.shape, 0)
                qpos = lax.broadcasted_iota(jnp.int32, st.shape, 1)
                st = jnp.where(kpos <= qpos, st, NEG)
            m_prev = m_ref[g]
            m_new = jnp.maximum(m_prev, jnp.max(st, axis=0, keepdims=True))
            p = jnp.exp2(st - m_new)
            alpha = jnp.exp2(m_prev - m_new)
            l_ref[g] = alpha * l_ref[g] + jnp.sum(p, axis=0, keepdims=True)
            pv = lax.dot_general(v_ref[pl.ds(k0, tk), heads[g]], p.astype(BF16),
                                 (((0,), (0,)), ((), ())),
                                 preferred_element_type=F32)
            acc_ref[g] = alpha * acc_ref[g] + pv
            m_ref[g] = m_new

    def body(kp, carry):
        block(pl.multiple_of(kp * (2 * tq), 2 * tq), 2 * tq, False)
        return carry

    lax.fori_loop(0, qi // 2, body, 0)

    @pl.when(qi % 2 == 1)
    def _():
        block(pl.multiple_of((qi - 1) * tq, tq), tq, False)

    block(pl.multiple_of(qi * tq, tq), tq, True)
    for g in range(n_g):
        o_ref[:, heads[g]] = (acc_ref[g] / l_ref[g]).T.astype(o_ref.dtype)


def _attention(q, k, v, nb, *, batch, lp, n_heads, tq=384, n_g=4):
    m = q.shape[0]
    nq = lp // tq
    gw = n_g * HEAD_DIM
    n_hg = n_heads // n_g
    nb4 = nb.reshape(batch * n_hg, n_g, lp // BLOCK, BLOCK)
    return pl.pallas_call(
        _attn_kernel,
        grid=(batch, n_hg, nq),
        in_specs=[
            pl.BlockSpec((tq, gw), lambda b, h, i: (b * nq + i, h)),
            pl.BlockSpec((lp, gw), lambda b, h, i: (b, h)),
            pl.BlockSpec((lp, gw), lambda b, h, i: (b, h)),
            pl.BlockSpec((1, n_g, lp // BLOCK, BLOCK), lambda b, h, i: (b * n_hg + h, 0, 0, 0)),
        ],
        out_specs=pl.BlockSpec((tq, gw), lambda b, h, i: (b * nq + i, h)),
        out_shape=jax.ShapeDtypeStruct((m, n_heads * HEAD_DIM), BF16),
        scratch_shapes=[
            pltpu.VMEM((n_g, HEAD_DIM, tq), BF16),
            pltpu.VMEM((n_g, lp, BLOCK), F32),
            pltpu.VMEM((n_g, 1, tq), F32),
            pltpu.VMEM((n_g, 1, tq), F32),
            pltpu.VMEM((n_g, HEAD_DIM, tq), F32),
            pltpu.VMEM((n_g, 2 * tq, tq), F32),
        ],
        compiler_params=_params("arbitrary", "arbitrary", "arbitrary"),
        name="forget_attn",
    )(q, k, v, nb4)


def kernel(x, meta, a_norm, a_w_in, a_conv, a_w_out, kv_norm, w_kv, k_norm, w_f, b_f,
           b_norm, b_w_q, b_q_norm, b_w_o, ffn_norm, ffn_w_gu, ffn_w_down):
    batch, seq, d = x.shape
    lp = PAD + N_META + seq
    n_a = a_w_in.shape[0]
    n_b = b_w_q.shape[0]
    n_heads = w_f.shape[1]
    assert lp % BLOCK == 0 and seq % RES_ROW_TILE == 0
    assert (batch * lp) % (2 * ROW_TILE) == 0 and (batch * lp) % RES_ROW_TILE == 0

    h, xn = _embed(x, meta)
    w_out_bf = a_w_out.astype(BF16)
    w_o_bf = b_w_o.astype(BF16)
    w_down_bf = ffn_w_down.astype(BF16)
    k = v = nb = None
    for layer in range(n_a + n_b):
        if layer < n_a:
            g = _conv_in(xn, a_norm[layer], a_w_in, layer, a_conv[layer])
            h, xn = _res_norm(g, w_out_bf, layer, h, name="conv_out")
        else:
            if layer == n_a:
                k, v = _kv_proj(xn, kv_norm, w_kv, k_norm)
                lf = _log_forget(h, kv_norm, w_f, b_f, lp=lp)
                nb = _key_bias(lf, batch=batch, lp=lp, n_heads=n_heads)
            jb = layer - n_a
            q = _q_proj(xn, b_norm[jb], b_w_q, jb, b_q_norm[jb])
            o = _attention(q, k, v, nb, batch=batch, lp=lp, n_heads=n_heads)
            h, xn = _res_norm(o, w_o_bf, jb, h, name="attn_out")
        act = _ffn_up(xn, ffn_norm[layer], ffn_w_gu, layer)
        last = layer == n_a + n_b - 1
        h, xn = _res_norm(act, w_down_bf, layer, h, name="ffn_down",
                          unpad=(batch, lp, seq) if last else None)
    return h.reshape(batch, seq, d)
```

```python
import functools
import math

import jax
import jax.numpy as jnp
from jax import lax
from jax.experimental import pallas as pl
from jax.experimental.pallas import tpu as pltpu

N_META = 16
CONV_WIDTH = 3
HEAD_DIM = 128
BLOCK = 128
PAD = BLOCK - N_META
EPS = 1e-6
NEG = -1e30
LOG2E = 1.4426950408889634

F32 = jnp.float32
BF16 = jnp.bfloat16

ROW_TILE = 768
RES_ROW_TILE = 512
COL_TILE = 512
MXU_COLS = 256
VMEM_LIMIT_BYTES = 56 * 1024 * 1024


def _params(*semantics):
    return pltpu.CompilerParams(dimension_semantics=semantics,
                                vmem_limit_bytes=VMEM_LIMIT_BYTES)


def _unit_rms(h):
    ms = jnp.mean(h * h, axis=-1, keepdims=True)
    return h * lax.rsqrt(ms + EPS)


def _head_rms(x, gain, scale):
    outs = []
    for hh in range(x.shape[1] // HEAD_DIM):
        xh = x[:, hh * HEAD_DIM:(hh + 1) * HEAD_DIM]
        ms = jnp.mean(xh * xh, axis=-1, keepdims=True)
        y = xh * lax.rsqrt(ms + EPS) * gain
        if scale is not None:
            y = y * scale
        outs.append(y)
    return outs


def _fold_gain(w_refs, g_ref, w_sc, fused=True):
    g = g_ref[...]
    n_s = len(w_refs)
    tn = w_refs[0].shape[1]
    grp = min(MXU_COLS, tn) if fused else tn
    for c in range(tn // grp):
        for s, w_ref in enumerate(w_refs):
            lo = (c * n_s + s) * grp
            w_sc[:, lo:lo + grp] = (w_ref[:, c * grp:(c + 1) * grp] * g).astype(BF16)


def _stream_dot(x, w_sc, n_s, fused=True):
    tn = w_sc.shape[1] // n_s
    if not fused:
        return [jnp.dot(x, w_sc[:, s * tn:(s + 1) * tn], preferred_element_type=F32)
                for s in range(n_s)]
    r = jnp.dot(x, w_sc[...], preferred_element_type=F32)
    grp = min(MXU_COLS, tn)
    return [jnp.concatenate([r[:, (c * n_s + s) * grp:(c * n_s + s + 1) * grp]
                             for c in range(tn // grp)], axis=1)
            for s in range(n_s)]


def _proj_call(kernel, xn, gain, w, layer, n_streams, n_out, extra_in, extra_specs,
               out_dtype, scratch, name, tm=ROW_TILE, tn=COL_TILE, n_results=1):
    m, k = xn.shape
    tn = min(tn, n_out)
    nj = n_out // tn
    w_specs = [pl.BlockSpec((None, k, tn),
                            functools.partial(lambda j, i, s: (layer, 0, s * nj + j), s=s))
               for s in range(n_streams)]
    out_spec = pl.BlockSpec((tm, tn), lambda j, i: (i, j))
    out_shape = jax.ShapeDtypeStruct((m, n_out), out_dtype)
    return pl.pallas_call(
        kernel,
        grid=(nj, m // tm),
        in_specs=[pl.BlockSpec((tm, k), lambda j, i: (i, 0)),
                  pl.BlockSpec((k, 1), lambda j, i: (0, 0))] + w_specs + extra_specs,
        out_specs=out_spec if n_results == 1 else [out_spec] * n_results,
        out_shape=out_shape if n_results == 1 else [out_shape] * n_results,
        scratch_shapes=[pltpu.VMEM((k, n_streams * tn), BF16)] + scratch(tm, tn),
        compiler_params=_params("arbitrary", "arbitrary"),
        name=name,
    )(xn, gain.reshape(k, 1), *([w] * n_streams), *extra_in)


def _embed_kernel(x_ref, meta_ref, h_ref, xn_ref):
    t = pl.program_id(1)

    rows = h_ref.shape[0]

    @pl.when(t == 0)
    def _():
        meta = meta_ref[...]
        head = x_ref[0:rows - BLOCK, :]
        h_ref[0:PAD, :] = jnp.zeros((PAD, h_ref.shape[1]), F32)
        h_ref[PAD:BLOCK, :] = meta
        h_ref[BLOCK:, :] = head
        xn_ref[0:PAD, :] = jnp.zeros((PAD, h_ref.shape[1]), BF16)
        xn_ref[PAD:BLOCK, :] = _unit_rms(meta).astype(BF16)
        xn_ref[BLOCK:, :] = _unit_rms(head).astype(BF16)

    @pl.when(t > 0)
    def _():
        body = x_ref[...]
        h_ref[...] = body
        xn_ref[...] = _unit_rms(body).astype(BF16)


def _embed(x, meta, *, rows=3 * BLOCK):
    batch, seq, d = x.shape
    lp = PAD + N_META + seq
    nb = lp // rows
    shape = (batch * lp, d)
    x_row0 = lambda t: pl.multiple_of(jnp.maximum(t * rows - BLOCK, 0), BLOCK)
    return pl.pallas_call(
        _embed_kernel,
        grid=(batch, nb),
        in_specs=[
            pl.BlockSpec((None, pl.Element(rows), pl.Element(d)),
                         lambda b, t: (b, x_row0(t), 0)),
            pl.BlockSpec((N_META, d), lambda b, t: (0, 0)),
        ],
        out_specs=[pl.BlockSpec((rows, d), lambda b, t: (b * nb + t, 0)),
                   pl.BlockSpec((rows, d), lambda b, t: (b * nb + t, 0))],
        out_shape=[jax.ShapeDtypeStruct(shape, F32), jax.ShapeDtypeStruct(shape, BF16)],
        compiler_params=_params("arbitrary", "arbitrary"),
        name="embed",
    )(x, meta.astype(x.dtype))


def _conv_in_kernel(x_ref, g_ref, wb_ref, wc_ref, wh_ref, cw_ref, o_ref,
                    w_sc, halo_ref, u_ref):
    tm = o_ref.shape[0]

    @pl.when(pl.program_id(1) == 0)
    def _():
        _fold_gain((wb_ref, wc_ref, wh_ref), g_ref, w_sc)
        halo_ref[...] = jnp.zeros(halo_ref.shape, F32)

    bg, cg, hh = _stream_dot(x_ref[...], w_sc, 3)
    u = cg * hh
    u_ref[0:8, :] = halo_ref[...]
    u_ref[8:, :] = u
    halo_ref[...] = u[tm - 8:, :]
    cw = cw_ref[...]
    conv = (u_ref[pl.ds(6, tm), :] * cw[0:1, :]
            + u_ref[pl.ds(7, tm), :] * cw[1:2, :]
            + u * cw[2:3, :])
    o_ref[...] = (bg * conv).astype(o_ref.dtype)


def _conv_in(xn, gain, w_in, layer, conv_w):
    d = xn.shape[1]
    tn = min(COL_TILE, d)
    return _proj_call(
        _conv_in_kernel, xn, gain, w_in, layer, 3, d,
        [conv_w], [pl.BlockSpec((CONV_WIDTH, tn), lambda j, i: (0, j))], BF16,
        lambda tm, tn: [pltpu.VMEM((8, tn), F32), pltpu.VMEM((tm + 8, tn), F32)],
        "conv_in")


def _ffn_up_kernel(x_ref, g_ref, wg_ref, wu_ref, o_ref, w_sc):
    @pl.when(pl.program_id(1) == 0)
    def _():
        _fold_gain((wg_ref, wu_ref), g_ref, w_sc)

    g, u = _stream_dot(x_ref[...], w_sc, 2)
    o_ref[...] = (g * jax.nn.sigmoid(g) * u).astype(o_ref.dtype)


def _ffn_up(xn, gain, w_gu, layer):
    return _proj_call(_ffn_up_kernel, xn, gain, w_gu, layer, 2, w_gu.shape[2] // 2,
                      [], [], BF16, lambda tm, tn: [], "ffn_up", tm=2 * ROW_TILE)


def _kv_kernel(x_ref, g_ref, wk_ref, wv_ref, kn_ref, k_ref, v_ref, w_sc):
    @pl.when(pl.program_id(1) == 0)
    def _():
        _fold_gain((wk_ref, wv_ref), g_ref, w_sc, fused=False)

    k, v = _stream_dot(x_ref[...], w_sc, 2, fused=False)
    for hh, kh in enumerate(_head_rms(k, kn_ref[...], None)):
        k_ref[:, hh * HEAD_DIM:(hh + 1) * HEAD_DIM] = kh.astype(k_ref.dtype)
    v_ref[...] = v.astype(v_ref.dtype)


def _kv_proj(xn, gain, w_kv, k_norm):
    return _proj_call(
        _kv_kernel, xn, gain, w_kv[None], 0, 2, w_kv.shape[1] // 2,
        [k_norm.reshape(1, HEAD_DIM)], [pl.BlockSpec((1, HEAD_DIM), lambda j, i: (0, 0))],
        BF16, lambda tm, tn: [], "kv_proj", n_results=2)


def _q_kernel(x_ref, g_ref, wq_ref, qn_ref, q_ref, w_sc):
    @pl.when(pl.program_id(1) == 0)
    def _():
        _fold_gain((wq_ref,), g_ref, w_sc)

    (q,) = _stream_dot(x_ref[...], w_sc, 1)
    scale = LOG2E / math.sqrt(HEAD_DIM)
    for hh, qh in enumerate(_head_rms(q, qn_ref[...], scale)):
        q_ref[:, hh * HEAD_DIM:(hh + 1) * HEAD_DIM] = qh.astype(q_ref.dtype)


def _q_proj(xn, gain, w_q, layer, q_norm):
    return _proj_call(
        _q_kernel, xn, gain, w_q, layer, 1, w_q.shape[2],
        [q_norm.reshape(1, HEAD_DIM)], [pl.BlockSpec((1, HEAD_DIM), lambda j, i: (0, 0))],
        BF16, lambda tm, tn: [], "q_proj", tn=2 * COL_TILE)


def _res_kernel(a_ref, w_ref, r_ref, h_ref, *xn_ref):
    h = r_ref[...] + jnp.dot(a_ref[...], w_ref[...], preferred_element_type=F32)
    h_ref[...] = h
    if xn_ref:
        xn_ref[0][...] = _unit_rms(h).astype(xn_ref[0].dtype)


def _res_call(a, w, layer, res, *, slab, n_slabs, emit_xn, name, unpad=None):
    kk = a.shape[1]
    n = w.shape[2]
    tm = RES_ROW_TILE
    tk = kk // n_slabs
    w_spec_kwargs = dict(pipeline_mode=pl.Buffered(1))
    if unpad is None:
        m = res.shape[0]
        grid = (m // tm,)
        a_spec = pl.BlockSpec((tm, tk), lambda i: (i, slab))
        w_spec = pl.BlockSpec((None, tk, n), lambda i: (layer, slab, 0), **w_spec_kwargs)
        r_spec = pl.BlockSpec((tm, n), lambda i: (i, 0))
        o_spec = pl.BlockSpec((tm, n), lambda i: (i, 0))
        semantics = ("arbitrary",)
    else:
        batch, lp, seq = unpad
        m = batch * seq
        nt = seq // tm
        grid = (batch, nt)
        row0 = lambda b, t: pl.multiple_of(b * lp + BLOCK + t * tm, BLOCK)
        a_spec = pl.BlockSpec((pl.Element(tm), pl.Element(tk)),
                              lambda b, t: (row0(b, t), slab * tk))
        w_spec = pl.BlockSpec((None, tk, n), lambda b, t: (layer, slab, 0), **w_spec_kwargs)
        r_spec = pl.BlockSpec((pl.Element(tm), pl.Element(n)),
                              lambda b, t: (row0(b, t), 0))
        o_spec = pl.BlockSpec((tm, n), lambda b, t: (b * nt + t, 0))
        semantics = ("arbitrary", "arbitrary")
    out_specs = [o_spec, o_spec] if emit_xn else o_spec
    out_shape = [jax.ShapeDtypeStruct((m, n), F32), jax.ShapeDtypeStruct((m, n), BF16)]
    return pl.pallas_call(
        _res_kernel,
        grid=grid,
        in_specs=[a_spec, w_spec, r_spec],
        out_specs=out_specs,
        out_shape=out_shape if emit_xn else out_shape[0],
        compiler_params=_params(*semantics),
        name=name,
    )(a, w, res)


def _res_norm(a, w, layer, res, *, name, unpad=None):
    kk = a.shape[1]
    n = w.shape[2]
    n_slabs = 1 if kk * n * 2 <= 12 * 1024 * 1024 else 2
    h, xn = res, None
    for slab in range(n_slabs):
        last = slab == n_slabs - 1
        out = _res_call(a, w, layer, h, slab=slab, n_slabs=n_slabs,
                        emit_xn=last and unpad is None, name=name,
                        unpad=unpad if last else None)
        h, xn = out if last and unpad is None else (out, None)
    return h, xn


def _logf_kernel(h_ref, g_ref, wf_ref, bf_ref, o_ref, *, lp):
    tm = h_ref.shape[0]
    xn = _unit_rms(h_ref[...]) * g_ref[...]
    hi = xn.astype(BF16)
    lo = (xn - hi.astype(F32)).astype(BF16)
    w2 = wf_ref[...]
    a = jnp.dot(hi, w2, preferred_element_type=F32)
    b = jnp.dot(lo, w2[:, 0:HEAD_DIM], preferred_element_type=F32)
    z = a[:, 0:HEAD_DIM] + a[:, HEAD_DIM:] + b + bf_ref[...]
    lf = jnp.minimum(z, 0.0) - jnp.log1p(jnp.exp(-jnp.abs(z)))
    row = pl.program_id(0) * tm + lax.broadcasted_iota(jnp.int32, lf.shape, 0)
    o_ref[...] = jnp.where(row % lp >= PAD, lf, 0.0)


def _log_forget(h, gain, w_f, b_f, *, lp, tm=256):
    m, d = h.shape
    nh = w_f.shape[1]
    wf = jnp.zeros((d, HEAD_DIM), F32).at[:, :nh].set(w_f)
    wf_hi = wf.astype(BF16)
    wf = jnp.concatenate([wf_hi, (wf - wf_hi.astype(F32)).astype(BF16)], axis=1)
    bf = jnp.zeros((1, HEAD_DIM), F32).at[0, :nh].set(b_f)
    return pl.pallas_call(
        functools.partial(_logf_kernel, lp=lp),
        grid=(m // tm,),
        in_specs=[
            pl.BlockSpec((tm, d), lambda i: (i, 0)),
            pl.BlockSpec((1, d), lambda i: (0, 0)),
            pl.BlockSpec((d, 2 * HEAD_DIM), lambda i: (0, 0)),
            pl.BlockSpec((1, HEAD_DIM), lambda i: (0, 0)),
        ],
        out_specs=pl.BlockSpec((tm, HEAD_DIM), lambda i: (i, 0)),
        out_shape=jax.ShapeDtypeStruct((m, HEAD_DIM), F32),
        compiler_params=_params("parallel"),
        name="log_forget",
    )(h, gain.reshape(1, d), wf, bf)


def _key_bias_kernel(lf_ref, nb_ref, carry_ref):
    ch = lf_ref.shape[0]
    nh = nb_ref.shape[1]
    t = pl.program_id(1)

    @pl.when(t == 0)
    def _():
        carry_ref[...] = jnp.zeros(carry_ref.shape, F32)

    xt = lf_ref[...].T
    s_idx = lax.broadcasted_iota(jnp.int32, (ch, ch), 0)
    t_idx = lax.broadcasted_iota(jnp.int32, (ch, ch), 1)
    tri = (s_idx <= t_idx).astype(F32)
    ct = jnp.dot(xt, tri, preferred_element_type=F32,
                 precision=lax.Precision.HIGHEST) + carry_ref[:, 0:1]
    carry_ref[...] = jnp.broadcast_to(ct[:, ch - 1:ch], carry_ref.shape)
    pos = t * ch + lax.broadcasted_iota(jnp.int32, (nh, ch), 1)
    nb_ref[0] = jnp.where(pos >= PAD, -LOG2E * ct[:nh, :], NEG)


def _key_bias(lf, *, batch, lp, n_heads, ch=384):
    nch = lp // ch
    return pl.pallas_call(
        _key_bias_kernel,
        grid=(batch, nch),
        in_specs=[pl.BlockSpec((ch, HEAD_DIM), lambda b, t: (b * nch + t, 0))],
        out_specs=pl.BlockSpec((1, n_heads, ch), lambda b, t: (b, 0, t)),
        out_shape=jax.ShapeDtypeStruct((batch, n_heads, lp), F32),
        scratch_shapes=[pltpu.VMEM((HEAD_DIM, HEAD_DIM), F32)],
        compiler_params=_params("arbitrary", "arbitrary"),
        name="key_bias",
    )(lf)


def _attn_kernel(q_ref, k_ref, v_ref, nb_ref, o_ref,
                 qt_ref, nbc_ref, m_ref, l_ref, acc_ref, s_ref):
    tq = q_ref.shape[0]
    n_g = q_ref.shape[1] // HEAD_DIM
    lp = k_ref.shape[0]
    qi = pl.program_id(2)
    heads = [slice(g * HEAD_DIM, (g + 1) * HEAD_DIM) for g in range(n_g)]

    @pl.when(qi == 0)
    def _():
        for g in range(n_g):
            for ch in range(lp // BLOCK):
                row = nb_ref[0, g, ch:ch + 1, :]
                nbc_ref[g, ch * BLOCK:(ch + 1) * BLOCK, :] = (
                    jnp.broadcast_to(row, (BLOCK, BLOCK)).T)

    for g in range(n_g):
        qt_ref[g] = q_ref[:, heads[g]].astype(F32).T.astype(BF16)
    m_ref[...] = jnp.full(m_ref.shape, NEG, F32)
    l_ref[...] = jnp.zeros(l_ref.shape, F32)
    acc_ref[...] = jnp.zeros(acc_ref.shape, F32)

    def block(k0, tk, diagonal):
        for g in range(n_g):
            s_ref[g, 0:tk, :] = jnp.dot(k_ref[pl.ds(k0, tk), heads[g]], qt_ref[g],
                                        preferred_element_type=F32)
        for g in range(n_g):
            nb = nbc_ref[g, pl.ds(k0, tk), :]
            st = s_ref[g, 0:tk, :] + jnp.concatenate([nb] * (tq // BLOCK), axis=1)
            if diagonal:
                kpos = lax.broadcasted_iota(jnp.int32, st.shape, 0)
                qpos = lax.broadcasted_iota(jnp.int32, st.shape, 1)
                st = jnp.where(kpos <= qpos, st, NEG)
            m_prev = m_ref[g]
            m_new = jnp.maximum(m_prev, jnp.max(st, axis=0, keepdims=True))
            p = jnp.exp2(st - m_new)
            alpha = jnp.exp2(m_prev - m_new)
            l_ref[g] = alpha * l_ref[g] + jnp.sum(p, axis=0, keepdims=True)
            pv = lax.dot_general(v_ref[pl.ds(k0, tk), heads[g]], p.astype(BF16),
                                 (((0,), (0,)), ((), ())),
                                 preferred_element_type=F32)
            acc_ref[g] = alpha * acc_ref[g] + pv
            m_ref[g] = m_new

    def body(kp, carry):
        block(pl.multiple_of(kp * (2 * tq), 2 * tq), 2 * tq, False)
        return carry

    lax.fori_loop(0, qi // 2, body, 0)

    @pl.when(qi % 2 == 1)
    def _():
        block(pl.multiple_of((qi - 1) * tq, tq), tq, False)

    block(pl.multiple_of(qi * tq, tq), tq, True)
    for g in range(n_g):
        o_ref[:, heads[g]] = (acc_ref[g] / l_ref[g]).T.astype(o_ref.dtype)


def _attention(q, k, v, nb, *, batch, lp, n_heads, tq=384, n_g=4):
    m = q.shape[0]
    nq = lp // tq
    gw = n_g * HEAD_DIM
    n_hg = n_heads // n_g
    nb4 = nb.reshape(batch * n_hg, n_g, lp // BLOCK, BLOCK)
    return pl.pallas_call(
        _attn_kernel,
        grid=(batch, n_hg, nq),
        in_specs=[
            pl.BlockSpec((tq, gw), lambda b, h, i: (b * nq + i, h)),
            pl.BlockSpec((lp, gw), lambda b, h, i: (b, h)),
            pl.BlockSpec((lp, gw), lambda b, h, i: (b, h)),
            pl.BlockSpec((1, n_g, lp // BLOCK, BLOCK), lambda b, h, i: (b * n_hg + h, 0, 0, 0)),
        ],
        out_specs=pl.BlockSpec((tq, gw), lambda b, h, i: (b * nq + i, h)),
        out_shape=jax.ShapeDtypeStruct((m, n_heads * HEAD_DIM), BF16),
        scratch_shapes=[
            pltpu.VMEM((n_g, HEAD_DIM, tq), BF16),
            pltpu.VMEM((n_g, lp, BLOCK), F32),
            pltpu.VMEM((n_g, 1, tq), F32),
            pltpu.VMEM((n_g, 1, tq), F32),
            pltpu.VMEM((n_g, HEAD_DIM, tq), F32),
            pltpu.VMEM((n_g, 2 * tq, tq), F32),
        ],
        compiler_params=_params("arbitrary", "arbitrary", "arbitrary"),
        name="forget_attn",
    )(q, k, v, nb4)


def kernel(x, meta, a_norm, a_w_in, a_conv, a_w_out, kv_norm, w_kv, k_norm, w_f, b_f,
           b_norm, b_w_q, b_q_norm, b_w_o, ffn_norm, ffn_w_gu, ffn_w_down):
    batch, seq, d = x.shape
    lp = PAD + N_META + seq
    n_a = a_w_in.shape[0]
    n_b = b_w_q.shape[0]
    n_heads = w_f.shape[1]
    assert lp % BLOCK == 0 and seq % RES_ROW_TILE == 0
    assert (batch * lp) % (2 * ROW_TILE) == 0 and (batch * lp) % RES_ROW_TILE == 0

    h, xn = _embed(x, meta)
    w_out_bf = a_w_out.astype(BF16)
    w_o_bf = b_w_o.astype(BF16)
    w_down_bf = ffn_w_down.astype(BF16)
    k = v = nb = None
    for layer in range(n_a + n_b):
        if layer < n_a:
            g = _conv_in(xn, a_norm[layer], a_w_in, layer, a_conv[layer])
            h, xn = _res_norm(g, w_out_bf, layer, h, name="conv_out")
        else:
            if layer == n_a:
                k, v = _kv_proj(xn, kv_norm, w_kv, k_norm)
                lf = _log_forget(h, kv_norm, w_f, b_f, lp=lp)
                nb = _key_bias(lf, batch=batch, lp=lp, n_heads=n_heads)
            jb = layer - n_a
            q = _q_proj(xn, b_norm[jb], b_w_q, jb, b_q_norm[jb])
            o = _attention(q, k, v, nb, batch=batch, lp=lp, n_heads=n_heads)
            h, xn = _res_norm(o, w_o_bf, jb, h, name="attn_out")
        act = _ffn_up(xn, ffn_norm[layer], ffn_w_gu, layer)
        last = layer == n_a + n_b - 1
        h, xn = _res_norm(act, w_down_bf, layer, h, name="ffn_down",
                          unpad=(batch, lp, seq) if last else None)
    return h.reshape(batch, seq, d)
```

```python
import functools
import math

import jax
import jax.numpy as jnp
from jax import lax
from jax.experimental import pallas as pl
from jax.experimental.pallas import tpu as pltpu

N_META = 16
CONV_WIDTH = 3
HEAD_DIM = 128
BLOCK = 128
PAD = BLOCK - N_META
EPS = 1e-6
NEG = -1e30
LOG2E = 1.4426950408889634

F32 = jnp.float32
BF16 = jnp.bfloat16

ROW_TILE = 768
CONV_ROW_TILE = 1056
FFN_ROW_TILE = 2112
RES_ROW_TILE = 512
COL_TILE = 512
MXU_COLS = 256
VMEM_LIMIT_BYTES = 56 * 1024 * 1024


def _params(*semantics):
    return pltpu.CompilerParams(dimension_semantics=semantics,
                                vmem_limit_bytes=VMEM_LIMIT_BYTES)


def _unit_rms(h):
    ms = jnp.mean(h * h, axis=-1, keepdims=True)
    return h * lax.rsqrt(ms + EPS)


def _head_rms(x, gain, scale):
    outs = []
    for hh in range(x.shape[1] // HEAD_DIM):
        xh = x[:, hh * HEAD_DIM:(hh + 1) * HEAD_DIM]
        ms = jnp.mean(xh * xh, axis=-1, keepdims=True)
        y = xh * lax.rsqrt(ms + EPS) * gain
        if scale is not None:
            y = y * scale
        outs.append(y)
    return outs


def _fold_gain(w_refs, g_ref, w_sc, fused=True):
    g = g_ref[...]
    n_s = len(w_refs)
    tn = w_refs[0].shape[1]
    grp = min(MXU_COLS, tn) if fused else tn
    for c in range(tn // grp):
        for s, w_ref in enumerate(w_refs):
            lo = (c * n_s + s) * grp
            w_sc[:, lo:lo + grp] = (w_ref[:, c * grp:(c + 1) * grp] * g).astype(BF16)


def _stream_dot(x, w_sc, n_s, fused=True):
    tn = w_sc.shape[1] // n_s
    if not fused:
        return [jnp.dot(x, w_sc[:, s * tn:(s + 1) * tn], preferred_element_type=F32)
                for s in range(n_s)]
    r = jnp.dot(x, w_sc[...], preferred_element_type=F32)
    grp = min(MXU_COLS, tn)
    return [jnp.concatenate([r[:, (c * n_s + s) * grp:(c * n_s + s + 1) * grp]
                             for c in range(tn // grp)], axis=1)
            for s in range(n_s)]


def _proj_call(kernel, xn, gain, w, layer, n_streams, n_out, extra_in, extra_specs,
               out_dtype, scratch, name, tm=ROW_TILE, tn=COL_TILE, n_results=1):
    m, k = xn.shape
    if m % tm:
        tm = ROW_TILE
    tn = min(tn, n_out)
    nj = n_out // tn
    w_specs = [pl.BlockSpec((None, k, tn),
                            functools.partial(lambda j, i, s: (layer, 0, s * nj + j), s=s))
               for s in range(n_streams)]
    out_spec = pl.BlockSpec((tm, tn), lambda j, i: (i, j))
    out_shape = jax.ShapeDtypeStruct((m, n_out), out_dtype)
    return pl.pallas_call(
        kernel,
        grid=(nj, m // tm),
        in_specs=[pl.BlockSpec((tm, k), lambda j, i: (i, 0)),
                  pl.BlockSpec((k, 1), lambda j, i: (0, 0))] + w_specs + extra_specs,
        out_specs=out_spec if n_results == 1 else [out_spec] * n_results,
        out_shape=out_shape if n_results == 1 else [out_shape] * n_results,
        scratch_shapes=[pltpu.VMEM((k, n_streams * tn), BF16)] + scratch(tm, tn),
        compiler_params=_params("arbitrary", "arbitrary"),
        name=name,
    )(xn, gain.reshape(k, 1), *([w] * n_streams), *extra_in)


def _embed_kernel(x_ref, meta_ref, h_ref, xn_ref):
    t = pl.program_id(1)

    rows = h_ref.shape[0]

    @pl.when(t == 0)
    def _():
        meta = meta_ref[...]
        head = x_ref[0:rows - BLOCK, :]
        h_ref[0:PAD, :] = jnp.zeros((PAD, h_ref.shape[1]), F32)
        h_ref[PAD:BLOCK, :] = meta
        h_ref[BLOCK:, :] = head
        xn_ref[0:PAD, :] = jnp.zeros((PAD, h_ref.shape[1]), BF16)
        xn_ref[PAD:BLOCK, :] = _unit_rms(meta).astype(BF16)
        xn_ref[BLOCK:, :] = _unit_rms(head).astype(BF16)

    @pl.when(t > 0)
    def _():
        body = x_ref[...]
        h_ref[...] = body
        xn_ref[...] = _unit_rms(body).astype(BF16)


def _embed(x, meta, *, rows=3 * BLOCK):
    batch, seq, d = x.shape
    lp = PAD + N_META + seq
    nb = lp // rows
    shape = (batch * lp, d)
    x_row0 = lambda t: pl.multiple_of(jnp.maximum(t * rows - BLOCK, 0), BLOCK)
    return pl.pallas_call(
        _embed_kernel,
        grid=(batch, nb),
        in_specs=[
            pl.BlockSpec((None, pl.Element(rows), pl.Element(d)),
                         lambda b, t: (b, x_row0(t), 0)),
            pl.BlockSpec((N_META, d), lambda b, t: (0, 0)),
        ],
        out_specs=[pl.BlockSpec((rows, d), lambda b, t: (b * nb + t, 0)),
                   pl.BlockSpec((rows, d), lambda b, t: (b * nb + t, 0))],
        out_shape=[jax.ShapeDtypeStruct(shape, F32), jax.ShapeDtypeStruct(shape, BF16)],
        compiler_params=_params("arbitrary", "arbitrary"),
        name="embed",
    )(x, meta.astype(x.dtype))


def _conv_in_kernel(x_ref, g_ref, wb_ref, wc_ref, wh_ref, cw_ref, o_ref,
                    w_sc, halo_ref, u_ref):
    tm = o_ref.shape[0]

    @pl.when(pl.program_id(1) == 0)
    def _():
        _fold_gain((wb_ref, wc_ref, wh_ref), g_ref, w_sc)
        halo_ref[...] = jnp.zeros(halo_ref.shape, F32)

    bg, cg, hh = _stream_dot(x_ref[...], w_sc, 3)
    u = cg * hh
    u_ref[0:8, :] = halo_ref[...]
    u_ref[8:, :] = u
    halo_ref[...] = u[tm - 8:, :]
    cw = cw_ref[...]
    conv = (u_ref[pl.ds(6, tm), :] * cw[0:1, :]
            + u_ref[pl.ds(7, tm), :] * cw[1:2, :]
            + u * cw[2:3, :])
    o_ref[...] = (bg * conv).astype(o_ref.dtype)


def _conv_in(xn, gain, w_in, layer, conv_w):
    d = xn.shape[1]
    tn = min(COL_TILE, d)
    return _proj_call(
        _conv_in_kernel, xn, gain, w_in, layer, 3, d,
        [conv_w], [pl.BlockSpec((CONV_WIDTH, tn), lambda j, i: (0, j))], BF16,
        lambda tm, tn: [pltpu.VMEM((8, tn), F32), pltpu.VMEM((tm + 8, tn), F32)],
        "conv_in", tm=CONV_ROW_TILE)


def _ffn_up_kernel(x_ref, g_ref, wg_ref, wu_ref, o_ref, w_sc):
    @pl.when(pl.program_id(1) == 0)
    def _():
        _fold_gain((wg_ref, wu_ref), g_ref, w_sc)

    g, u = _stream_dot(x_ref[...], w_sc, 2)
    o_ref[...] = (g * jax.nn.sigmoid(g) * u).astype(o_ref.dtype)


def _ffn_up(xn, gain, w_gu, layer):
    return _proj_call(_ffn_up_kernel, xn, gain, w_gu, layer, 2, w_gu.shape[2] // 2,
                      [], [], BF16, lambda tm, tn: [], "ffn_up", tm=FFN_ROW_TILE)


def _kv_kernel(x_ref, g_ref, wk_ref, wv_ref, kn_ref, k_ref, v_ref, w_sc):
    @pl.when(pl.program_id(1) == 0)
    def _():
        _fold_gain((wk_ref, wv_ref), g_ref, w_sc, fused=False)

    k, v = _stream_dot(x_ref[...], w_sc, 2, fused=False)
    for hh, kh in enumerate(_head_rms(k, kn_ref[...], None)):
        k_ref[:, hh * HEAD_DIM:(hh + 1) * HEAD_DIM] = kh.astype(k_ref.dtype)
    v_ref[...] = v.astype(v_ref.dtype)


def _kv_proj(xn, gain, w_kv, k_norm):
    return _proj_call(
        _kv_kernel, xn, gain, w_kv[None], 0, 2, w_kv.shape[1] // 2,
        [k_norm.reshape(1, HEAD_DIM)], [pl.BlockSpec((1, HEAD_DIM), lambda j, i: (0, 0))],
        BF16, lambda tm, tn: [], "kv_proj", tm=2 * ROW_TILE, n_results=2)


def _q_kernel(x_ref, g_ref, wq_ref, qn_ref, q_ref, w_sc):
    @pl.when(pl.program_id(1) == 0)
    def _():
        _fold_gain((wq_ref,), g_ref, w_sc)

    (q,) = _stream_dot(x_ref[...], w_sc, 1)
    scale = LOG2E / math.sqrt(HEAD_DIM)
    for hh, qh in enumerate(_head_rms(q, qn_ref[...], scale)):
        q_ref[:, hh * HEAD_DIM:(hh + 1) * HEAD_DIM] = qh.astype(q_ref.dtype)


def _q_proj(xn, gain, w_q, layer, q_norm):
    return _proj_call(
        _q_kernel, xn, gain, w_q, layer, 1, w_q.shape[2],
        [q_norm.reshape(1, HEAD_DIM)], [pl.BlockSpec((1, HEAD_DIM), lambda j, i: (0, 0))],
        BF16, lambda tm, tn: [], "q_proj", tn=2 * COL_TILE)


def _res_kernel(a_ref, w_ref, r_ref, h_ref, *xn_ref):
    h = r_ref[...] + jnp.dot(a_ref[...], w_ref[...], preferred_element_type=F32)
    h_ref[...] = h
    if xn_ref:
        xn_ref[0][...] = _unit_rms(h).astype(xn_ref[0].dtype)


def _res_call(a, w, layer, res, *, slab, n_slabs, emit_xn, name, unpad=None):
    kk = a.shape[1]
    n = w.shape[2]
    tm = RES_ROW_TILE
    tk = kk // n_slabs
    w_spec_kwargs = dict(pipeline_mode=pl.Buffered(1))
    if unpad is None:
        m = res.shape[0]
        grid = (m // tm,)
        a_spec = pl.BlockSpec((tm, tk), lambda i: (i, slab))
        w_spec = pl.BlockSpec((None, tk, n), lambda i: (layer, slab, 0), **w_spec_kwargs)
        r_spec = pl.BlockSpec((tm, n), lambda i: (i, 0))
        o_spec = pl.BlockSpec((tm, n), lambda i: (i, 0))
        semantics = ("arbitrary",)
    else:
        batch, lp, seq = unpad
        m = batch * seq
        nt = seq // tm
        grid = (batch, nt)
        row0 = lambda b, t: pl.multiple_of(b * lp + BLOCK + t * tm, BLOCK)
        a_spec = pl.BlockSpec((pl.Element(tm), pl.Element(tk)),
                              lambda b, t: (row0(b, t), slab * tk))
        w_spec = pl.BlockSpec((None, tk, n), lambda b, t: (layer, slab, 0), **w_spec_kwargs)
        r_spec = pl.BlockSpec((pl.Element(tm), pl.Element(n)),
                              lambda b, t: (row0(b, t), 0))
        o_spec = pl.BlockSpec((tm, n), lambda b, t: (b * nt + t, 0))
        semantics = ("arbitrary", "arbitrary")
    out_specs = [o_spec, o_spec] if emit_xn else o_spec
    out_shape = [jax.ShapeDtypeStruct((m, n), F32), jax.ShapeDtypeStruct((m, n), BF16)]
    return pl.pallas_call(
        _res_kernel,
        grid=grid,
        in_specs=[a_spec, w_spec, r_spec],
        out_specs=out_specs,
        out_shape=out_shape if emit_xn else out_shape[0],
        compiler_params=_params(*semantics),
        name=name,
    )(a, w, res)


def _res_norm(a, w, layer, res, *, name, unpad=None):
    kk = a.shape[1]
    n = w.shape[2]
    n_slabs = 1 if kk * n * 2 <= 12 * 1024 * 1024 else 2
    h, xn = res, None
    for slab in range(n_slabs):
        last = slab == n_slabs - 1
        out = _res_call(a, w, layer, h, slab=slab, n_slabs=n_slabs,
                        emit_xn=last and unpad is None, name=name,
                        unpad=unpad if last else None)
        h, xn = out if last and unpad is None else (out, None)
    return h, xn


def _logf_kernel(h_ref, g_ref, wf_ref, bf_ref, o_ref, *, lp):
    tm = h_ref.shape[0]
    xn = _unit_rms(h_ref[...]) * g_ref[...]
    hi = xn.astype(BF16)
    lo = (xn - hi.astype(F32)).astype(BF16)
    w2 = wf_ref[...]
    a = jnp.dot(hi, w2, preferred_element_type=F32)
    b = jnp.dot(lo, w2[:, 0:HEAD_DIM], preferred_element_type=F32)
    z = a[:, 0:HEAD_DIM] + a[:, HEAD_DIM:] + b + bf_ref[...]
    lf = jnp.minimum(z, 0.0) - jnp.log1p(jnp.exp(-jnp.abs(z)))
    row = pl.program_id(0) * tm + lax.broadcasted_iota(jnp.int32, lf.shape, 0)
    o_ref[...] = jnp.where(row % lp >= PAD, lf, 0.0)


def _log_forget(h, gain, w_f, b_f, *, lp, tm=256):
    m, d = h.shape
    nh = w_f.shape[1]
    wf = jnp.zeros((d, HEAD_DIM), F32).at[:, :nh].set(w_f)
    wf_hi = wf.astype(BF16)
    wf = jnp.concatenate([wf_hi, (wf - wf_hi.astype(F32)).astype(BF16)], axis=1)
    bf = jnp.zeros((1, HEAD_DIM), F32).at[0, :nh].set(b_f)
    return pl.pallas_call(
        functools.partial(_logf_kernel, lp=lp),
        grid=(m // tm,),
        in_specs=[
            pl.BlockSpec((tm, d), lambda i: (i, 0)),
            pl.BlockSpec((1, d), lambda i: (0, 0)),
            pl.BlockSpec((d, 2 * HEAD_DIM), lambda i: (0, 0)),
            pl.BlockSpec((1, HEAD_DIM), lambda i: (0, 0)),
        ],
        out_specs=pl.BlockSpec((tm, HEAD_DIM), lambda i: (i, 0)),
        out_shape=jax.ShapeDtypeStruct((m, HEAD_DIM), F32),
        compiler_params=_params("parallel"),
        name="log_forget",
    )(h, gain.reshape(1, d), wf, bf)


def _key_bias_kernel(lf_ref, nb_ref, carry_ref):
    ch = lf_ref.shape[0]
    nh = nb_ref.shape[1]
    t = pl.program_id(1)

    @pl.when(t == 0)
    def _():
        carry_ref[...] = jnp.zeros(carry_ref.shape, F32)

    xt = lf_ref[...].T
    s_idx = lax.broadcasted_iota(jnp.int32, (ch, ch), 0)
    t_idx = lax.broadcasted_iota(jnp.int32, (ch, ch), 1)
    tri = (s_idx <= t_idx).astype(F32)
    ct = jnp.dot(xt, tri, preferred_element_type=F32,
                 precision=lax.Precision.HIGHEST) + carry_ref[:, 0:1]
    carry_ref[...] = jnp.broadcast_to(ct[:, ch - 1:ch], carry_ref.shape)
    pos = t * ch + lax.broadcasted_iota(jnp.int32, (nh, ch), 1)
    nb_ref[0] = jnp.where(pos >= PAD, -LOG2E * ct[:nh, :], NEG)


def _key_bias(lf, *, batch, lp, n_heads, ch=384):
    nch = lp // ch
    return pl.pallas_call(
        _key_bias_kernel,
        grid=(batch, nch),
        in_specs=[pl.BlockSpec((ch, HEAD_DIM), lambda b, t: (b * nch + t, 0))],
        out_specs=pl.BlockSpec((1, n_heads, ch), lambda b, t: (b, 0, t)),
        out_shape=jax.ShapeDtypeStruct((batch, n_heads, lp), F32),
        scratch_shapes=[pltpu.VMEM((HEAD_DIM, HEAD_DIM), F32)],
        compiler_params=_params("arbitrary", "arbitrary"),
        name="key_bias",
    )(lf)


def _attn_kernel(q_ref, k_ref, v_ref, nb_ref, o_ref,
                 qt_ref, nbc_ref, m_ref, l_ref, acc_ref, s_ref):
    tq = q_ref.shape[0]
    n_g = q_ref.shape[1] // HEAD_DIM
    lp = k_ref.shape[0]
    qi = pl.program_id(2)
    heads = [slice(g * HEAD_DIM, (g + 1) * HEAD_DIM) for g in range(n_g)]

    @pl.when(qi == 0)
    def _():
        for g in range(n_g):
            for ch in range(lp // BLOCK):
                row = nb_ref[0, g, ch:ch + 1, :]
                nbc_ref[g, ch * BLOCK:(ch + 1) * BLOCK, :] = (
                    jnp.broadcast_to(row, (BLOCK, BLOCK)).T)

    for g in range(n_g):
        qt_ref[g] = q_ref[:, heads[g]].astype(F32).T.astype(BF16)
    m_ref[...] = jnp.full(m_ref.shape, NEG, F32)
    l_ref[...] = jnp.zeros(l_ref.shape, F32)
    acc_ref[...] = jnp.zeros(acc_ref.shape, F32)

    def block(k0, tk, diagonal):
        for g in range(n_g):
            s_ref[g, 0:tk, :] = jnp.dot(k_ref[pl.ds(k0, tk), heads[g]], qt_ref[g],
                                        preferred_element_type=F32)
        for g in range(n_g):
            nb = nbc_ref[g, pl.ds(k0, tk), :]
            st = s_ref[g, 0:tk, :] + jnp.concatenate([nb] * (tq // BLOCK), axis=1)
            if diagonal:
                kpos = lax.broadcasted_iota(jnp.int32, st.shape, 0)
                qpos = lax.broadcasted_iota(jnp.int32, st.shape, 1)
                st = jnp.where(kpos <= qpos, st, NEG)
            m_prev = m_ref[g]
            m_new = jnp.maximum(m_prev, jnp.max(st, axis=0, keepdims=True))
            p = jnp.exp2(st - m_new)
            alpha = jnp.exp2(m_prev - m_new)
            l_ref[g] = alpha * l_ref[g] + jnp.sum(p, axis=0, keepdims=True)
            pv = lax.dot_general(v_ref[pl.ds(k0, tk), heads[g]], p.astype(BF16),
                                 (((0,), (0,)), ((), ())),
                                 preferred_element_type=F32)
            acc_ref[g] = alpha * acc_ref[g] + pv
            m_ref[g] = m_new

    def body(kp, carry):
        block(pl.multiple_of(kp * (2 * tq), 2 * tq), 2 * tq, False)
        return carry

    lax.fori_loop(0, qi // 2, body, 0)

    @pl.when(qi % 2 == 1)
    def _():
        block(pl.multiple_of((qi - 1) * tq, tq), tq, False)

    block(pl.multiple_of(qi * tq, tq), tq, True)
    for g in range(n_g):
        o_ref[:, heads[g]] = (acc_ref[g] / l_ref[g]).T.astype(o_ref.dtype)


def _attention(q, k, v, nb, *, batch, lp, n_heads, tq=384, n_g=4):
    m = q.shape[0]
    nq = lp // tq
    gw = n_g * HEAD_DIM
    n_hg = n_heads // n_g
    nb4 = nb.reshape(batch * n_hg, n_g, lp // BLOCK, BLOCK)
    return pl.pallas_call(
        _attn_kernel,
        grid=(batch, n_hg, nq),
        in_specs=[
            pl.BlockSpec((tq, gw), lambda b, h, i: (b * nq + i, h)),
            pl.BlockSpec((lp, gw), lambda b, h, i: (b, h)),
            pl.BlockSpec((lp, gw), lambda b, h, i: (b, h)),
            pl.BlockSpec((1, n_g, lp // BLOCK, BLOCK), lambda b, h, i: (b * n_hg + h, 0, 0, 0)),
        ],
        out_specs=pl.BlockSpec((tq, gw), lambda b, h, i: (b * nq + i, h)),
        out_shape=jax.ShapeDtypeStruct((m, n_heads * HEAD_DIM), BF16),
        scratch_shapes=[
            pltpu.VMEM((n_g, HEAD_DIM, tq), BF16),
            pltpu.VMEM((n_g, lp, BLOCK), F32),
            pltpu.VMEM((n_g, 1, tq), F32),
            pltpu.VMEM((n_g, 1, tq), F32),
            pltpu.VMEM((n_g, HEAD_DIM, tq), F32),
            pltpu.VMEM((n_g, 2 * tq, tq), F32),
        ],
        compiler_params=_params("arbitrary", "arbitrary", "arbitrary"),
        name="forget_attn",
    )(q, k, v, nb4)


def kernel(x, meta, a_norm, a_w_in, a_conv, a_w_out, kv_norm, w_kv, k_norm, w_f, b_f,
           b_norm, b_w_q, b_q_norm, b_w_o, ffn_norm, ffn_w_gu, ffn_w_down):
    batch, seq, d = x.shape
    lp = PAD + N_META + seq
    n_a = a_w_in.shape[0]
    n_b = b_w_q.shape[0]
    n_heads = w_f.shape[1]
    assert lp % BLOCK == 0 and seq % RES_ROW_TILE == 0
    assert (batch * lp) % (2 * ROW_TILE) == 0 and (batch * lp) % RES_ROW_TILE == 0

    h, xn = _embed(x, meta)
    w_out_bf = a_w_out.astype(BF16)
    w_o_bf = b_w_o.astype(BF16)
    w_down_bf = ffn_w_down.astype(BF16)
    k = v = nb = None
    for layer in range(n_a + n_b):
        if layer < n_a:
            g = _conv_in(xn, a_norm[layer], a_w_in, layer, a_conv[layer])
            h, xn = _res_norm(g, w_out_bf, layer, h, name="conv_out")
        else:
            if layer == n_a:
                k, v = _kv_proj(xn, kv_norm, w_kv, k_norm)
                lf = _log_forget(h, kv_norm, w_f, b_f, lp=lp)
                nb = _key_bias(lf, batch=batch, lp=lp, n_heads=n_heads)
            jb = layer - n_a
            q = _q_proj(xn, b_norm[jb], b_w_q, jb, b_q_norm[jb])
            o = _attention(q, k, v, nb, batch=batch, lp=lp, n_heads=n_heads)
            h, xn = _res_norm(o, w_o_bf, jb, h, name="attn_out")
        act = _ffn_up(xn, ffn_norm[layer], ffn_w_gu, layer)
        last = layer == n_a + n_b - 1
        h, xn = _res_norm(act, w_down_bf, layer, h, name="ffn_down",
                          unpad=(batch, lp, seq) if last else None)
    return h.reshape(batch, seq, d)
```

```python
import functools
import math

import jax
import jax.numpy as jnp
from jax import lax
from jax.experimental import pallas as pl
from jax.experimental.pallas import tpu as pltpu

N_META = 16
CONV_WIDTH = 3
HEAD_DIM = 128
BLOCK = 128
PAD = BLOCK - N_META
EPS = 1e-6
NEG = -1e30
LOG2E = 1.4426950408889634

F32 = jnp.float32
BF16 = jnp.bfloat16

ROW_TILE = 768
CONV_ROW_TILE = 1056
FFN_ROW_TILE = 2112
RES_ROW_TILE = 512
COL_TILE = 512
MXU_COLS = 256
VMEM_LIMIT_BYTES = 56 * 1024 * 1024


def _params(*semantics):
    return pltpu.CompilerParams(dimension_semantics=semantics,
                                vmem_limit_bytes=VMEM_LIMIT_BYTES)


def _unit_rms(h):
    ms = jnp.mean(h * h, axis=-1, keepdims=True)
    return h * lax.rsqrt(ms + EPS)


def _head_rms(x, gain, scale):
    outs = []
    for hh in range(x.shape[1] // HEAD_DIM):
        xh = x[:, hh * HEAD_DIM:(hh + 1) * HEAD_DIM]
        ms = jnp.mean(xh * xh, axis=-1, keepdims=True)
        y = xh * lax.rsqrt(ms + EPS) * gain
        if scale is not None:
            y = y * scale
        outs.append(y)
    return outs


def _fold_gain(w_refs, g_ref, w_sc, fused=True):
    g = g_ref[...]
    n_s = len(w_refs)
    tn = w_refs[0].shape[1]
    grp = min(MXU_COLS, tn) if fused else tn
    for c in range(tn // grp):
        for s, w_ref in enumerate(w_refs):
            lo = (c * n_s + s) * grp
            w_sc[:, lo:lo + grp] = (w_ref[:, c * grp:(c + 1) * grp] * g).astype(BF16)


def _stream_dot(x, w_sc, n_s, fused=True):
    tn = w_sc.shape[1] // n_s
    if not fused:
        return [jnp.dot(x, w_sc[:, s * tn:(s + 1) * tn], preferred_element_type=F32)
                for s in range(n_s)]
    r = jnp.dot(x, w_sc[...], preferred_element_type=F32)
    grp = min(MXU_COLS, tn)
    return [jnp.concatenate([r[:, (c * n_s + s) * grp:(c * n_s + s + 1) * grp]
                             for c in range(tn // grp)], axis=1)
            for s in range(n_s)]


def _proj_call(kernel, xn, gain, w, layer, n_streams, n_out, extra_in, extra_specs,
               out_dtype, scratch, name, tm=ROW_TILE, tn=COL_TILE, n_results=1):
    m, k = xn.shape
    if m % tm:
        tm = ROW_TILE
    tn = min(tn, n_out)
    nj = n_out // tn
    w_specs = [pl.BlockSpec((None, k, tn),
                            functools.partial(lambda j, i, s: (layer, 0, s * nj + j), s=s))
               for s in range(n_streams)]
    out_spec = pl.BlockSpec((tm, tn), lambda j, i: (i, j))
    out_shape = jax.ShapeDtypeStruct((m, n_out), out_dtype)
    return pl.pallas_call(
        kernel,
        grid=(nj, m // tm),
        in_specs=[pl.BlockSpec((tm, k), lambda j, i: (i, 0)),
                  pl.BlockSpec((k, 1), lambda j, i: (0, 0))] + w_specs + extra_specs,
        out_specs=out_spec if n_results == 1 else [out_spec] * n_results,
        out_shape=out_shape if n_results == 1 else [out_shape] * n_results,
        scratch_shapes=[pltpu.VMEM((k, n_streams * tn), BF16)] + scratch(tm, tn),
        compiler_params=_params("arbitrary", "arbitrary"),
        name=name,
    )(xn, gain.reshape(k, 1), *([w] * n_streams), *extra_in)


def _embed_kernel(x_ref, meta_ref, h_ref, xn_ref):
    t = pl.program_id(1)

    rows = h_ref.shape[0]

    @pl.when(t == 0)
    def _():
        meta = meta_ref[...]
        head = x_ref[0:rows - BLOCK, :]
        h_ref[0:PAD, :] = jnp.zeros((PAD, h_ref.shape[1]), F32)
        h_ref[PAD:BLOCK, :] = meta
        h_ref[BLOCK:, :] = head
        xn_ref[0:PAD, :] = jnp.zeros((PAD, h_ref.shape[1]), BF16)
        xn_ref[PAD:BLOCK, :] = _unit_rms(meta).astype(BF16)
        xn_ref[BLOCK:, :] = _unit_rms(head).astype(BF16)

    @pl.when(t > 0)
    def _():
        body = x_ref[...]
        h_ref[...] = body
        xn_ref[...] = _unit_rms(body).astype(BF16)


def _embed(x, meta, *, rows=3 * BLOCK):
    batch, seq, d = x.shape
    lp = PAD + N_META + seq
    nb = lp // rows
    shape = (batch * lp, d)
    x_row0 = lambda t: pl.multiple_of(jnp.maximum(t * rows - BLOCK, 0), BLOCK)
    return pl.pallas_call(
        _embed_kernel,
        grid=(batch, nb),
        in_specs=[
            pl.BlockSpec((None, pl.Element(rows), pl.Element(d)),
                         lambda b, t: (b, x_row0(t), 0)),
            pl.BlockSpec((N_META, d), lambda b, t: (0, 0)),
        ],
        out_specs=[pl.BlockSpec((rows, d), lambda b, t: (b * nb + t, 0)),
                   pl.BlockSpec((rows, d), lambda b, t: (b * nb + t, 0))],
        out_shape=[jax.ShapeDtypeStruct(shape, F32), jax.ShapeDtypeStruct(shape, BF16)],
        compiler_params=_params("arbitrary", "arbitrary"),
        name="embed",
    )(x, meta.astype(x.dtype))


def _conv_in_kernel(x_ref, g_ref, wb_ref, wc_ref, wh_ref, cw_ref, o_ref,
                    w_sc, halo_ref, u_ref):
    tm = o_ref.shape[0]

    @pl.when(pl.program_id(1) == 0)
    def _():
        _fold_gain((wb_ref, wc_ref, wh_ref), g_ref, w_sc)
        halo_ref[...] = jnp.zeros(halo_ref.shape, F32)

    bg, cg, hh = _stream_dot(x_ref[...], w_sc, 3)
    u = cg * hh
    u_ref[0:8, :] = halo_ref[...]
    u_ref[8:, :] = u
    halo_ref[...] = u[tm - 8:, :]
    cw = cw_ref[...]
    conv = (u_ref[pl.ds(6, tm), :] * cw[0:1, :]
            + u_ref[pl.ds(7, tm), :] * cw[1:2, :]
            + u * cw[2:3, :])
    o_ref[...] = (bg * conv).astype(o_ref.dtype)


def _conv_in(xn, gain, w_in, layer, conv_w):
    d = xn.shape[1]
    tn = min(COL_TILE, d)
    return _proj_call(
        _conv_in_kernel, xn, gain, w_in, layer, 3, d,
        [conv_w], [pl.BlockSpec((CONV_WIDTH, tn), lambda j, i: (0, j))], BF16,
        lambda tm, tn: [pltpu.VMEM((8, tn), F32), pltpu.VMEM((tm + 8, tn), F32)],
        "conv_in", tm=CONV_ROW_TILE)


def _ffn_up_kernel(x_ref, g_ref, wg_ref, wu_ref, o_ref, w_sc):
    @pl.when(pl.program_id(1) == 0)
    def _():
        _fold_gain((wg_ref, wu_ref), g_ref, w_sc)

    g, u = _stream_dot(x_ref[...], w_sc, 2)
    o_ref[...] = (g * jax.nn.sigmoid(g) * u).astype(o_ref.dtype)


def _ffn_up(xn, gain, w_gu, layer):
    return _proj_call(_ffn_up_kernel, xn, gain, w_gu, layer, 2, w_gu.shape[2] // 2,
                      [], [], BF16, lambda tm, tn: [], "ffn_up", tm=FFN_ROW_TILE)


def _kv_kernel(x_ref, g_ref, wk_ref, wv_ref, kn_ref, k_ref, v_ref, w_sc):
    @pl.when(pl.program_id(1) == 0)
    def _():
        _fold_gain((wk_ref, wv_ref), g_ref, w_sc, fused=False)

    k, v = _stream_dot(x_ref[...], w_sc, 2, fused=False)
    for hh, kh in enumerate(_head_rms(k, kn_ref[...], None)):
        k_ref[:, hh * HEAD_DIM:(hh + 1) * HEAD_DIM] = kh.astype(k_ref.dtype)
    v_ref[...] = v.astype(v_ref.dtype)


def _kv_proj(xn, gain, w_kv, k_norm):
    return _proj_call(
        _kv_kernel, xn, gain, w_kv[None], 0, 2, w_kv.shape[1] // 2,
        [k_norm.reshape(1, HEAD_DIM)], [pl.BlockSpec((1, HEAD_DIM), lambda j, i: (0, 0))],
        BF16, lambda tm, tn: [], "kv_proj", tm=2 * ROW_TILE, n_results=2)


def _q_kernel(x_ref, g_ref, wq_ref, qn_ref, q_ref, w_sc):
    @pl.when(pl.program_id(1) == 0)
    def _():
        _fold_gain((wq_ref,), g_ref, w_sc)

    (q,) = _stream_dot(x_ref[...], w_sc, 1)
    scale = LOG2E / math.sqrt(HEAD_DIM)
    for hh, qh in enumerate(_head_rms(q, qn_ref[...], scale)):
        q_ref[:, hh * HEAD_DIM:(hh + 1) * HEAD_DIM] = qh.astype(q_ref.dtype)


def _q_proj(xn, gain, w_q, layer, q_norm):
    return _proj_call(
        _q_kernel, xn, gain, w_q, layer, 1, w_q.shape[2],
        [q_norm.reshape(1, HEAD_DIM)], [pl.BlockSpec((1, HEAD_DIM), lambda j, i: (0, 0))],
        BF16, lambda tm, tn: [], "q_proj", tn=2 * COL_TILE)


def _res_kernel(a_ref, w_ref, r_ref, h_ref, *xn_ref):
    h = r_ref[...] + jnp.dot(a_ref[...], w_ref[...], preferred_element_type=F32)
    h_ref[...] = h
    if xn_ref:
        xn_ref[0][...] = _unit_rms(h).astype(xn_ref[0].dtype)


def _res_call(a, w, layer, res, *, slab, n_slabs, emit_xn, name, unpad=None):
    kk = a.shape[1]
    n = w.shape[2]
    tm = RES_ROW_TILE
    tk = kk // n_slabs
    w_spec_kwargs = dict(pipeline_mode=pl.Buffered(1))
    if unpad is None:
        m = res.shape[0]
        grid = (m // tm,)
        a_spec = pl.BlockSpec((tm, tk), lambda i: (i, slab))
        w_spec = pl.BlockSpec((None, tk, n), lambda i: (layer, slab, 0), **w_spec_kwargs)
        r_spec = pl.BlockSpec((tm, n), lambda i: (i, 0))
        o_spec = pl.BlockSpec((tm, n), lambda i: (i, 0))
        semantics = ("arbitrary",)
    else:
        batch, lp, seq = unpad
        m = batch * seq
        nt = seq // tm
        grid = (batch, nt)
        row0 = lambda b, t: pl.multiple_of(b * lp + BLOCK + t * tm, BLOCK)
        a_spec = pl.BlockSpec((pl.Element(tm), pl.Element(tk)),
                              lambda b, t: (row0(b, t), slab * tk))
        w_spec = pl.BlockSpec((None, tk, n), lambda b, t: (layer, slab, 0), **w_spec_kwargs)
        r_spec = pl.BlockSpec((pl.Element(tm), pl.Element(n)),
                              lambda b, t: (row0(b, t), 0))
        o_spec = pl.BlockSpec((tm, n), lambda b, t: (b * nt + t, 0))
        semantics = ("arbitrary", "arbitrary")
    out_specs = [o_spec, o_spec] if emit_xn else o_spec
    out_shape = [jax.ShapeDtypeStruct((m, n), F32), jax.ShapeDtypeStruct((m, n), BF16)]
    return pl.pallas_call(
        _res_kernel,
        grid=grid,
        in_specs=[a_spec, w_spec, r_spec],
        out_specs=out_specs,
        out_shape=out_shape if emit_xn else out_shape[0],
        compiler_params=_params(*semantics),
        name=name,
    )(a, w, res)


def _res_norm(a, w, layer, res, *, name, unpad=None):
    kk = a.shape[1]
    n = w.shape[2]
    n_slabs = 1 if kk * n * 2 <= 12 * 1024 * 1024 else 2
    h, xn = res, None
    for slab in range(n_slabs):
        last = slab == n_slabs - 1
        out = _res_call(a, w, layer, h, slab=slab, n_slabs=n_slabs,
                        emit_xn=last and unpad is None, name=name,
                        unpad=unpad if last else None)
        h, xn = out if last and unpad is None else (out, None)
    return h, xn


def _logf_kernel(h_ref, g_ref, wf_ref, bf_ref, o_ref, *, lp):
    tm = h_ref.shape[0]
    xn = _unit_rms(h_ref[...]) * g_ref[...]
    hi = xn.astype(BF16)
    lo = (xn - hi.astype(F32)).astype(BF16)
    w2 = wf_ref[...]
    a = jnp.dot(hi, w2, preferred_element_type=F32)
    b = jnp.dot(lo, w2[:, 0:HEAD_DIM], preferred_element_type=F32)
    z = a[:, 0:HEAD_DIM] + a[:, HEAD_DIM:] + b + bf_ref[...]
    lf = jnp.minimum(z, 0.0) - jnp.log1p(jnp.exp(-jnp.abs(z)))
    row = pl.program_id(0) * tm + lax.broadcasted_iota(jnp.int32, lf.shape, 0)
    o_ref[...] = jnp.where(row % lp >= PAD, lf, 0.0)


def _log_forget(h, gain, w_f, b_f, *, lp, tm=256):
    m, d = h.shape
    nh = w_f.shape[1]
    wf = jnp.zeros((d, HEAD_DIM), F32).at[:, :nh].set(w_f)
    wf_hi = wf.astype(BF16)
    wf = jnp.concatenate([wf_hi, (wf - wf_hi.astype(F32)).astype(BF16)], axis=1)
    bf = jnp.zeros((1, HEAD_DIM), F32).at[0, :nh].set(b_f)
    return pl.pallas_call(
        functools.partial(_logf_kernel, lp=lp),
        grid=(m // tm,),
        in_specs=[
            pl.BlockSpec((tm, d), lambda i: (i, 0)),
            pl.BlockSpec((1, d), lambda i: (0, 0)),
            pl.BlockSpec((d, 2 * HEAD_DIM), lambda i: (0, 0)),
            pl.BlockSpec((1, HEAD_DIM), lambda i: (0, 0)),
        ],
        out_specs=pl.BlockSpec((tm, HEAD_DIM), lambda i: (i, 0)),
        out_shape=jax.ShapeDtypeStruct((m, HEAD_DIM), F32),
        compiler_params=_params("parallel"),
        name="log_forget",
    )(h, gain.reshape(1, d), wf, bf)


def _key_bias_kernel(lf_ref, nb_ref, carry_ref):
    ch = lf_ref.shape[0]
    nh = nb_ref.shape[1]
    t = pl.program_id(1)

    @pl.when(t == 0)
    def _():
        carry_ref[...] = jnp.zeros(carry_ref.shape, F32)

    xt = lf_ref[...].T
    s_idx = lax.broadcasted_iota(jnp.int32, (ch, ch), 0)
    t_idx = lax.broadcasted_iota(jnp.int32, (ch, ch), 1)
    tri = (s_idx <= t_idx).astype(F32)
    ct = jnp.dot(xt, tri, preferred_element_type=F32,
                 precision=lax.Precision.HIGHEST) + carry_ref[:, 0:1]
    carry_ref[...] = jnp.broadcast_to(ct[:, ch - 1:ch], carry_ref.shape)
    pos = t * ch + lax.broadcasted_iota(jnp.int32, (nh, ch), 1)
    nb_ref[0] = jnp.where(pos >= PAD, -LOG2E * ct[:nh, :], NEG)


def _key_bias(lf, *, batch, lp, n_heads, ch=384):
    nch = lp // ch
    return pl.pallas_call(
        _key_bias_kernel,
        grid=(batch, nch),
        in_specs=[pl.BlockSpec((ch, HEAD_DIM), lambda b, t: (b * nch + t, 0))],
        out_specs=pl.BlockSpec((1, n_heads, ch), lambda b, t: (b, 0, t)),
        out_shape=jax.ShapeDtypeStruct((batch, n_heads, lp), F32),
        scratch_shapes=[pltpu.VMEM((HEAD_DIM, HEAD_DIM), F32)],
        compiler_params=_params("arbitrary", "arbitrary"),
        name="key_bias",
    )(lf)


def _attn_kernel(q_ref, k_ref, v_ref, nb_ref, o_ref,
                 qt_ref, nbc_ref, m_ref, l_ref, acc_ref, s_ref):
    tq = q_ref.shape[0]
    n_g = q_ref.shape[1] // HEAD_DIM
    lp = k_ref.shape[0]
    qi = pl.program_id(2)
    heads = [slice(g * HEAD_DIM, (g + 1) * HEAD_DIM) for g in range(n_g)]

    @pl.when(qi == 0)
    def _():
        for g in range(n_g):
            for ch in range(lp // BLOCK):
                row = nb_ref[0, g, ch:ch + 1, :]
                nbc_ref[g, ch * BLOCK:(ch + 1) * BLOCK, :] = (
                    jnp.broadcast_to(row, (BLOCK, BLOCK)).T)

    for g in range(n_g):
        qt_ref[g] = q_ref[:, heads[g]].astype(F32).T.astype(BF16)
    m_ref[...] = jnp.full(m_ref.shape, NEG, F32)
    l_ref[...] = jnp.zeros(l_ref.shape, F32)
    acc_ref[...] = jnp.zeros(acc_ref.shape, F32)

    def block(k0, tk, diagonal):
        for g in range(n_g):
            nb = nbc_ref[g, pl.ds(k0, tk), :]
            s_ref[g, 0:tk, :] = jnp.concatenate([nb] * (tq // BLOCK), axis=1) + jnp.dot(
                k_ref[pl.ds(k0, tk), heads[g]], qt_ref[g],
                preferred_element_type=F32)
        for g in range(n_g):
            st = s_ref[g, 0:tk, :]
            if diagonal:
                kpos = lax.broadcasted_iota(jnp.int32, st.shape, 0)
                qpos = lax.broadcasted_iota(jnp.int32, st.shape, 1)
                st = jnp.where(kpos <= qpos, st, NEG)
            m_prev = m_ref[g]
            m_new = jnp.maximum(m_prev, jnp.max(st, axis=0, keepdims=True))
            p = jnp.exp2(st - m_new)
            alpha = jnp.exp2(m_prev - m_new)
            l_ref[g] = alpha * l_ref[g] + jnp.sum(p, axis=0, keepdims=True)
            pv = lax.dot_general(v_ref[pl.ds(k0, tk), heads[g]], p.astype(BF16),
                                 (((0,), (0,)), ((), ())),
                                 preferred_element_type=F32)
            acc_ref[g] = alpha * acc_ref[g] + pv
            m_ref[g] = m_new

    def body(kp, carry):
        block(pl.multiple_of(kp * (3 * tq), 3 * tq), 3 * tq, False)
        return carry

    n3 = qi // 3
    lax.fori_loop(0, n3, body, 0)
    rem = qi - 3 * n3

    @pl.when(rem == 2)
    def _():
        block(pl.multiple_of((qi - 2) * tq, tq), 2 * tq, False)

    @pl.when(rem == 1)
    def _():
        block(pl.multiple_of((qi - 1) * tq, tq), tq, False)

    block(pl.multiple_of(qi * tq, tq), tq, True)
    for g in range(n_g):
        o_ref[:, heads[g]] = (acc_ref[g] / l_ref[g]).T.astype(o_ref.dtype)


def _attention(q, k, v, nb, *, batch, lp, n_heads, tq=384, n_g=4):
    m = q.shape[0]
    nq = lp // tq
    gw = n_g * HEAD_DIM
    n_hg = n_heads // n_g
    nb4 = nb.reshape(batch * n_hg, n_g, lp // BLOCK, BLOCK)
    return pl.pallas_call(
        _attn_kernel,
        grid=(batch, n_hg, nq),
        in_specs=[
            pl.BlockSpec((tq, gw), lambda b, h, i: (b * nq + i, h)),
            pl.BlockSpec((lp, gw), lambda b, h, i: (b, h)),
            pl.BlockSpec((lp, gw), lambda b, h, i: (b, h)),
            pl.BlockSpec((1, n_g, lp // BLOCK, BLOCK), lambda b, h, i: (b * n_hg + h, 0, 0, 0)),
        ],
        out_specs=pl.BlockSpec((tq, gw), lambda b, h, i: (b * nq + i, h)),
        out_shape=jax.ShapeDtypeStruct((m, n_heads * HEAD_DIM), BF16),
        scratch_shapes=[
            pltpu.VMEM((n_g, HEAD_DIM, tq), BF16),
            pltpu.VMEM((n_g, lp, BLOCK), F32),
            pltpu.VMEM((n_g, 1, tq), F32),
            pltpu.VMEM((n_g, 1, tq), F32),
            pltpu.VMEM((n_g, HEAD_DIM, tq), F32),
            pltpu.VMEM((n_g, 3 * tq, tq), F32),
        ],
        compiler_params=_params("arbitrary", "arbitrary", "arbitrary"),
        name="forget_attn",
    )(q, k, v, nb4)


def kernel(x, meta, a_norm, a_w_in, a_conv, a_w_out, kv_norm, w_kv, k_norm, w_f, b_f,
           b_norm, b_w_q, b_q_norm, b_w_o, ffn_norm, ffn_w_gu, ffn_w_down):
    batch, seq, d = x.shape
    lp = PAD + N_META + seq
    n_a = a_w_in.shape[0]
    n_b = b_w_q.shape[0]
    n_heads = w_f.shape[1]
    assert lp % BLOCK == 0 and seq % RES_ROW_TILE == 0
    assert (batch * lp) % (2 * ROW_TILE) == 0 and (batch * lp) % RES_ROW_TILE == 0

    h, xn = _embed(x, meta)
    w_out_bf = a_w_out.astype(BF16)
    w_o_bf = b_w_o.astype(BF16)
    w_down_bf = ffn_w_down.astype(BF16)
    k = v = nb = None
    for layer in range(n_a + n_b):
        if layer < n_a:
            g = _conv_in(xn, a_norm[layer], a_w_in, layer, a_conv[layer])
            h, xn = _res_norm(g, w_out_bf, layer, h, name="conv_out")
        else:
            if layer == n_a:
                k, v = _kv_proj(xn, kv_norm, w_kv, k_norm)
                lf = _log_forget(h, kv_norm, w_f, b_f, lp=lp)
                nb = _key_bias(lf, batch=batch, lp=lp, n_heads=n_heads)
            jb = layer - n_a
            q = _q_proj(xn, b_norm[jb], b_w_q, jb, b_q_norm[jb])
            o = _attention(q, k, v, nb, batch=batch, lp=lp, n_heads=n_heads)
            h, xn = _res_norm(o, w_o_bf, jb, h, name="attn_out")
        act = _ffn_up(xn, ffn_norm[layer], ffn_w_gu, layer)
        last = layer == n_a + n_b - 1
        h, xn = _res_norm(act, w_down_bf, layer, h, name="ffn_down",
                          unpad=(batch, lp, seq) if last else None)
    return h.reshape(batch, seq, d)
```

```python
import functools
import math

import jax
import jax.numpy as jnp
from jax import lax
from jax.experimental import pallas as pl
from jax.experimental.pallas import tpu as pltpu

N_META = 16
CONV_WIDTH = 3
HEAD_DIM = 128
BLOCK = 128
PAD = BLOCK - N_META
EPS = 1e-6
NEG = -1e30
LOG2E = 1.4426950408889634

F32 = jnp.float32
BF16 = jnp.bfloat16

ROW_TILE = 768
CONV_ROW_TILE = 1056
FFN_ROW_TILE = 2112
RES_ROW_TILE = 512
COL_TILE = 512
MXU_COLS = 256
VMEM_LIMIT_BYTES = 56 * 1024 * 1024


def _params(*semantics):
    return pltpu.CompilerParams(dimension_semantics=semantics,
                                vmem_limit_bytes=VMEM_LIMIT_BYTES)


def _unit_rms(h):
    ms = jnp.mean(h * h, axis=-1, keepdims=True)
    return h * lax.rsqrt(ms + EPS)


def _head_rms(x, gain, scale):
    outs = []
    for hh in range(x.shape[1] // HEAD_DIM):
        xh = x[:, hh * HEAD_DIM:(hh + 1) * HEAD_DIM]
        ms = jnp.mean(xh * xh, axis=-1, keepdims=True)
        y = xh * lax.rsqrt(ms + EPS) * gain
        if scale is not None:
            y = y * scale
        outs.append(y)
    return outs


def _fold_gain(w_refs, g_ref, w_sc, fused=True):
    g = g_ref[...]
    n_s = len(w_refs)
    tn = w_refs[0].shape[1]
    grp = min(MXU_COLS, tn) if fused else tn
    for c in range(tn // grp):
        for s, w_ref in enumerate(w_refs):
            lo = (c * n_s + s) * grp
            w_sc[:, lo:lo + grp] = (w_ref[:, c * grp:(c + 1) * grp] * g).astype(BF16)


def _stream_dot(x, w_sc, n_s, fused=True):
    tn = w_sc.shape[1] // n_s
    if not fused:
        return [jnp.dot(x, w_sc[:, s * tn:(s + 1) * tn], preferred_element_type=F32)
                for s in range(n_s)]
    r = jnp.dot(x, w_sc[...], preferred_element_type=F32)
    grp = min(MXU_COLS, tn)
    return [jnp.concatenate([r[:, (c * n_s + s) * grp:(c * n_s + s + 1) * grp]
                             for c in range(tn // grp)], axis=1)
            for s in range(n_s)]


def _proj_call(kernel, xn, gain, w, layer, n_streams, n_out, extra_in, extra_specs,
               out_dtype, scratch, name, tm=ROW_TILE, tn=COL_TILE, n_results=1):
    m, k = xn.shape
    if m % tm:
        tm = ROW_TILE
    tn = min(tn, n_out)
    nj = n_out // tn
    w_specs = [pl.BlockSpec((None, k, tn),
                            functools.partial(lambda j, i, s: (layer, 0, s * nj + j), s=s))
               for s in range(n_streams)]
    out_spec = pl.BlockSpec((tm, tn), lambda j, i: (i, j))
    out_shape = jax.ShapeDtypeStruct((m, n_out), out_dtype)
    return pl.pallas_call(
        kernel,
        grid=(nj, m // tm),
        in_specs=[pl.BlockSpec((tm, k), lambda j, i: (i, 0)),
                  pl.BlockSpec((k, 1), lambda j, i: (0, 0))] + w_specs + extra_specs,
        out_specs=out_spec if n_results == 1 else [out_spec] * n_results,
        out_shape=out_shape if n_results == 1 else [out_shape] * n_results,
        scratch_shapes=[pltpu.VMEM((k, n_streams * tn), BF16)] + scratch(tm, tn),
        compiler_params=_params("arbitrary", "arbitrary"),
        name=name,
    )(xn, gain.reshape(k, 1), *([w] * n_streams), *extra_in)


def _embed_kernel(x_ref, meta_ref, h_ref, xn_ref):
    t = pl.program_id(1)

    rows = h_ref.shape[0]

    @pl.when(t == 0)
    def _():
        meta = meta_ref[...]
        head = x_ref[0:rows - BLOCK, :]
        h_ref[0:PAD, :] = jnp.zeros((PAD, h_ref.shape[1]), F32)
        h_ref[PAD:BLOCK, :] = meta
        h_ref[BLOCK:, :] = head
        xn_ref[0:PAD, :] = jnp.zeros((PAD, h_ref.shape[1]), BF16)
        xn_ref[PAD:BLOCK, :] = _unit_rms(meta).astype(BF16)
        xn_ref[BLOCK:, :] = _unit_rms(head).astype(BF16)

    @pl.when(t > 0)
    def _():
        body = x_ref[...]
        h_ref[...] = body
        xn_ref[...] = _unit_rms(body).astype(BF16)


def _embed(x, meta, *, rows=3 * BLOCK):
    batch, seq, d = x.shape
    lp = PAD + N_META + seq
    nb = lp // rows
    shape = (batch * lp, d)
    x_row0 = lambda t: pl.multiple_of(jnp.maximum(t * rows - BLOCK, 0), BLOCK)
    return pl.pallas_call(
        _embed_kernel,
        grid=(batch, nb),
        in_specs=[
            pl.BlockSpec((None, pl.Element(rows), pl.Element(d)),
                         lambda b, t: (b, x_row0(t), 0)),
            pl.BlockSpec((N_META, d), lambda b, t: (0, 0)),
        ],
        out_specs=[pl.BlockSpec((rows, d), lambda b, t: (b * nb + t, 0)),
                   pl.BlockSpec((rows, d), lambda b, t: (b * nb + t, 0))],
        out_shape=[jax.ShapeDtypeStruct(shape, F32), jax.ShapeDtypeStruct(shape, BF16)],
        compiler_params=_params("arbitrary", "arbitrary"),
        name="embed",
    )(x, meta.astype(x.dtype))


def _conv_in_kernel(x_ref, g_ref, wb_ref, wc_ref, wh_ref, cw_ref, o_ref,
                    w_sc, halo_ref, u_ref):
    tm = o_ref.shape[0]

    @pl.when(pl.program_id(1) == 0)
    def _():
        _fold_gain((wb_ref, wc_ref, wh_ref), g_ref, w_sc)
        halo_ref[...] = jnp.zeros(halo_ref.shape, F32)

    bg, cg, hh = _stream_dot(x_ref[...], w_sc, 3)
    u = cg * hh
    u_ref[0:8, :] = halo_ref[...]
    u_ref[8:, :] = u
    halo_ref[...] = u[tm - 8:, :]
    cw = cw_ref[...]
    conv = (u_ref[pl.ds(6, tm), :] * cw[0:1, :]
            + u_ref[pl.ds(7, tm), :] * cw[1:2, :]
            + u * cw[2:3, :])
    o_ref[...] = (bg * conv).astype(o_ref.dtype)


def _conv_in(xn, gain, w_in, layer, conv_w):
    d = xn.shape[1]
    tn = min(COL_TILE, d)
    return _proj_call(
        _conv_in_kernel, xn, gain, w_in, layer, 3, d,
        [conv_w], [pl.BlockSpec((CONV_WIDTH, tn), lambda j, i: (0, j))], BF16,
        lambda tm, tn: [pltpu.VMEM((8, tn), F32), pltpu.VMEM((tm + 8, tn), F32)],
        "conv_in", tm=CONV_ROW_TILE)


def _ffn_up_kernel(x_ref, g_ref, wg_ref, wu_ref, o_ref, w_sc):
    @pl.when(pl.program_id(1) == 0)
    def _():
        _fold_gain((wg_ref, wu_ref), g_ref, w_sc)

    g, u = _stream_dot(x_ref[...], w_sc, 2)
    o_ref[...] = (g * jax.nn.sigmoid(g) * u).astype(o_ref.dtype)


def _ffn_up(xn, gain, w_gu, layer):
    return _proj_call(_ffn_up_kernel, xn, gain, w_gu, layer, 2, w_gu.shape[2] // 2,
                      [], [], BF16, lambda tm, tn: [], "ffn_up", tm=FFN_ROW_TILE)


def _kv_kernel(x_ref, g_ref, wk_ref, wv_ref, kn_ref, k_ref, v_ref, w_sc):
    @pl.when(pl.program_id(1) == 0)
    def _():
        _fold_gain((wk_ref, wv_ref), g_ref, w_sc, fused=False)

    k, v = _stream_dot(x_ref[...], w_sc, 2, fused=False)
    for hh, kh in enumerate(_head_rms(k, kn_ref[...], None)):
        k_ref[:, hh * HEAD_DIM:(hh + 1) * HEAD_DIM] = kh.astype(k_ref.dtype)
    v_ref[...] = v.astype(v_ref.dtype)


def _kv_proj(xn, gain, w_kv, k_norm):
    return _proj_call(
        _kv_kernel, xn, gain, w_kv[None], 0, 2, w_kv.shape[1] // 2,
        [k_norm.reshape(1, HEAD_DIM)], [pl.BlockSpec((1, HEAD_DIM), lambda j, i: (0, 0))],
        BF16, lambda tm, tn: [], "kv_proj", tm=2 * ROW_TILE, n_results=2)


def _q_kernel(x_ref, g_ref, wq_ref, qn_ref, q_ref, w_sc):
    @pl.when(pl.program_id(1) == 0)
    def _():
        _fold_gain((wq_ref,), g_ref, w_sc)

    (q,) = _stream_dot(x_ref[...], w_sc, 1)
    scale = LOG2E / math.sqrt(HEAD_DIM)
    for hh, qh in enumerate(_head_rms(q, qn_ref[...], scale)):
        q_ref[:, hh * HEAD_DIM:(hh + 1) * HEAD_DIM] = qh.astype(q_ref.dtype)


def _q_proj(xn, gain, w_q, layer, q_norm):
    return _proj_call(
        _q_kernel, xn, gain, w_q, layer, 1, w_q.shape[2],
        [q_norm.reshape(1, HEAD_DIM)], [pl.BlockSpec((1, HEAD_DIM), lambda j, i: (0, 0))],
        BF16, lambda tm, tn: [], "q_proj", tn=2 * COL_TILE)


def _res_kernel(a_ref, w_ref, r_ref, h_ref, *xn_ref):
    h = r_ref[...] + jnp.dot(a_ref[...], w_ref[...], preferred_element_type=F32)
    h_ref[...] = h
    if xn_ref:
        xn_ref[0][...] = _unit_rms(h).astype(xn_ref[0].dtype)


def _res_call(a, w, layer, res, *, slab, n_slabs, emit_xn, name, unpad=None):
    kk = a.shape[1]
    n = w.shape[2]
    tm = RES_ROW_TILE
    tk = kk // n_slabs
    w_spec_kwargs = dict(pipeline_mode=pl.Buffered(1))
    if unpad is None:
        m = res.shape[0]
        grid = (m // tm,)
        a_spec = pl.BlockSpec((tm, tk), lambda i: (i, slab))
        w_spec = pl.BlockSpec((None, tk, n), lambda i: (layer, slab, 0), **w_spec_kwargs)
        r_spec = pl.BlockSpec((tm, n), lambda i: (i, 0))
        o_spec = pl.BlockSpec((tm, n), lambda i: (i, 0))
        semantics = ("arbitrary",)
    else:
        batch, lp, seq = unpad
        m = batch * seq
        nt = seq // tm
        grid = (batch, nt)
        row0 = lambda b, t: pl.multiple_of(b * lp + BLOCK + t * tm, BLOCK)
        a_spec = pl.BlockSpec((pl.Element(tm), pl.Element(tk)),
                              lambda b, t: (row0(b, t), slab * tk))
        w_spec = pl.BlockSpec((None, tk, n), lambda b, t: (layer, slab, 0), **w_spec_kwargs)
        r_spec = pl.BlockSpec((pl.Element(tm), pl.Element(n)),
                              lambda b, t: (row0(b, t), 0))
        o_spec = pl.BlockSpec((tm, n), lambda b, t: (b * nt + t, 0))
        semantics = ("arbitrary", "arbitrary")
    out_specs = [o_spec, o_spec] if emit_xn else o_spec
    out_shape = [jax.ShapeDtypeStruct((m, n), F32), jax.ShapeDtypeStruct((m, n), BF16)]
    return pl.pallas_call(
        _res_kernel,
        grid=grid,
        in_specs=[a_spec, w_spec, r_spec],
        out_specs=out_specs,
        out_shape=out_shape if emit_xn else out_shape[0],
        compiler_params=_params(*semantics),
        name=name,
    )(a, w, res)


def _res_norm(a, w, layer, res, *, name, unpad=None):
    kk = a.shape[1]
    n = w.shape[2]
    n_slabs = 1 if kk * n * 2 <= 12 * 1024 * 1024 else 2
    h, xn = res, None
    for slab in range(n_slabs):
        last = slab == n_slabs - 1
        out = _res_call(a, w, layer, h, slab=slab, n_slabs=n_slabs,
                        emit_xn=last and unpad is None, name=name,
                        unpad=unpad if last else None)
        h, xn = out if last and unpad is None else (out, None)
    return h, xn


def _key_bias_kernel(h_ref, g_ref, wf_ref, bf_ref, nb_ref, carry_ref):
    ch = h_ref.shape[0]
    nh = nb_ref.shape[1]
    t = pl.program_id(1)

    @pl.when(t == 0)
    def _():
        carry_ref[...] = jnp.zeros(carry_ref.shape, F32)

    xn = _unit_rms(h_ref[...]) * g_ref[...]
    hi = xn.astype(BF16)
    lo = (xn - hi.astype(F32)).astype(BF16)
    w2 = wf_ref[...]
    a = jnp.dot(hi, w2, preferred_element_type=F32)
    b = jnp.dot(lo, w2[:, 0:HEAD_DIM], preferred_element_type=F32)
    z = a[:, 0:HEAD_DIM] + a[:, HEAD_DIM:] + b + bf_ref[...]
    lf = jnp.minimum(z, 0.0) - jnp.log1p(jnp.exp(-jnp.abs(z)))
    row = t * ch + lax.broadcasted_iota(jnp.int32, lf.shape, 0)
    lf = jnp.where(row >= PAD, lf, 0.0)

    xt = lf.T
    s_idx = lax.broadcasted_iota(jnp.int32, (ch, ch), 0)
    t_idx = lax.broadcasted_iota(jnp.int32, (ch, ch), 1)
    tri = (s_idx <= t_idx).astype(F32)
    ct = jnp.dot(xt, tri, preferred_element_type=F32,
                 precision=lax.Precision.HIGHEST) + carry_ref[:, 0:1]
    carry_ref[...] = jnp.broadcast_to(ct[:, ch - 1:ch], carry_ref.shape)
    pos = t * ch + lax.broadcasted_iota(jnp.int32, (nh, ch), 1)
    nb_ref[0] = jnp.where(pos >= PAD, -LOG2E * ct[:nh, :], NEG)


def _key_bias(h, gain, w_f, b_f, *, batch, lp, ch=384):
    d = h.shape[1]
    n_heads = w_f.shape[1]
    nch = lp // ch
    wf = jnp.zeros((d, HEAD_DIM), F32).at[:, :n_heads].set(w_f)
    wf_hi = wf.astype(BF16)
    wf = jnp.concatenate([wf_hi, (wf - wf_hi.astype(F32)).astype(BF16)], axis=1)
    bf = jnp.zeros((1, HEAD_DIM), F32).at[0, :n_heads].set(b_f)
    return pl.pallas_call(
        _key_bias_kernel,
        grid=(batch, nch),
        in_specs=[
            pl.BlockSpec((ch, d), lambda b, t: (b * nch + t, 0)),
            pl.BlockSpec((1, d), lambda b, t: (0, 0)),
            pl.BlockSpec((d, 2 * HEAD_DIM), lambda b, t: (0, 0)),
            pl.BlockSpec((1, HEAD_DIM), lambda b, t: (0, 0)),
        ],
        out_specs=pl.BlockSpec((1, n_heads, ch), lambda b, t: (b, 0, t)),
        out_shape=jax.ShapeDtypeStruct((batch, n_heads, lp), F32),
        scratch_shapes=[pltpu.VMEM((HEAD_DIM, HEAD_DIM), F32)],
        compiler_params=_params("arbitrary", "arbitrary"),
        name="key_bias",
    )(h, gain.reshape(1, d), wf, bf)


def _attn_kernel(q_ref, k_ref, v_ref, nb_ref, o_ref,
                 qt_ref, nbc_ref, m_ref, l_ref, acc_ref, s_ref):
    tq = q_ref.shape[0]
    n_g = q_ref.shape[1] // HEAD_DIM
    lp = k_ref.shape[0]
    qi = pl.program_id(2)
    heads = [slice(g * HEAD_DIM, (g + 1) * HEAD_DIM) for g in range(n_g)]

    @pl.when(qi == 0)
    def _():
        for g in range(n_g):
            for ch in range(lp // BLOCK):
                row = nb_ref[0, g, ch:ch + 1, :]
                nbc_ref[g, ch * BLOCK:(ch + 1) * BLOCK, :] = (
                    jnp.broadcast_to(row, (BLOCK, BLOCK)).T)

    for g in range(n_g):
        qt_ref[g] = q_ref[:, heads[g]].astype(F32).T.astype(BF16)
    m_ref[...] = jnp.full(m_ref.shape, NEG, F32)
    l_ref[...] = jnp.zeros(l_ref.shape, F32)
    acc_ref[...] = jnp.zeros(acc_ref.shape, F32)

    def block(k0, tk, diagonal):
        for g in range(n_g):
            nb = nbc_ref[g, pl.ds(k0, tk), :]
            s_ref[g, 0:tk, :] = jnp.concatenate([nb] * (tq // BLOCK), axis=1) + jnp.dot(
                k_ref[pl.ds(k0, tk), heads[g]], qt_ref[g],
                preferred_element_type=F32)
        for g in range(n_g):
            st = s_ref[g, 0:tk, :]
            if diagonal:
                kpos = lax.broadcasted_iota(jnp.int32, st.shape, 0)
                qpos = lax.broadcasted_iota(jnp.int32, st.shape, 1)
                st = jnp.where(kpos <= qpos, st, NEG)
            m_prev = m_ref[g]
            m_new = jnp.maximum(m_prev, jnp.max(st, axis=0, keepdims=True))
            p = jnp.exp2(st - m_new)
            alpha = jnp.exp2(m_prev - m_new)
            l_ref[g] = alpha * l_ref[g] + jnp.sum(p, axis=0, keepdims=True)
            pv = lax.dot_general(v_ref[pl.ds(k0, tk), heads[g]], p.astype(BF16),
                                 (((0,), (0,)), ((), ())),
                                 preferred_element_type=F32)
            acc_ref[g] = alpha * acc_ref[g] + pv
            m_ref[g] = m_new

    def body(kp, carry):
        block(pl.multiple_of(kp * (3 * tq), 3 * tq), 3 * tq, False)
        return carry

    n3 = qi // 3
    lax.fori_loop(0, n3, body, 0)
    rem = qi - 3 * n3

    @pl.when(rem == 2)
    def _():
        block(pl.multiple_of((qi - 2) * tq, tq), 2 * tq, False)

    @pl.when(rem == 1)
    def _():
        block(pl.multiple_of((qi - 1) * tq, tq), tq, False)

    block(pl.multiple_of(qi * tq, tq), tq, True)
    for g in range(n_g):
        o_ref[:, heads[g]] = (acc_ref[g] / l_ref[g]).T.astype(o_ref.dtype)


def _attention(q, k, v, nb, *, batch, lp, n_heads, tq=384, n_g=4):
    m = q.shape[0]
    nq = lp // tq
    gw = n_g * HEAD_DIM
    n_hg = n_heads // n_g
    nb4 = nb.reshape(batch * n_hg, n_g, lp // BLOCK, BLOCK)
    return pl.pallas_call(
        _attn_kernel,
        grid=(batch, n_hg, nq),
        in_specs=[
            pl.BlockSpec((tq, gw), lambda b, h, i: (b * nq + i, h)),
            pl.BlockSpec((lp, gw), lambda b, h, i: (b, h)),
            pl.BlockSpec((lp, gw), lambda b, h, i: (b, h)),
            pl.BlockSpec((1, n_g, lp // BLOCK, BLOCK), lambda b, h, i: (b * n_hg + h, 0, 0, 0)),
        ],
        out_specs=pl.BlockSpec((tq, gw), lambda b, h, i: (b * nq + i, h)),
        out_shape=jax.ShapeDtypeStruct((m, n_heads * HEAD_DIM), BF16),
        scratch_shapes=[
            pltpu.VMEM((n_g, HEAD_DIM, tq), BF16),
            pltpu.VMEM((n_g, lp, BLOCK), F32),
            pltpu.VMEM((n_g, 1, tq), F32),
            pltpu.VMEM((n_g, 1, tq), F32),
            pltpu.VMEM((n_g, HEAD_DIM, tq), F32),
            pltpu.VMEM((n_g, 3 * tq, tq), F32),
        ],
        compiler_params=_params("arbitrary", "arbitrary", "arbitrary"),
        name="forget_attn",
    )(q, k, v, nb4)


def kernel(x, meta, a_norm, a_w_in, a_conv, a_w_out, kv_norm, w_kv, k_norm, w_f, b_f,
           b_norm, b_w_q, b_q_norm, b_w_o, ffn_norm, ffn_w_gu, ffn_w_down):
    batch, seq, d = x.shape
    lp = PAD + N_META + seq
    n_a = a_w_in.shape[0]
    n_b = b_w_q.shape[0]
    n_heads = w_f.shape[1]
    assert lp % BLOCK == 0 and seq % RES_ROW_TILE == 0
    assert (batch * lp) % (2 * ROW_TILE) == 0 and (batch * lp) % RES_ROW_TILE == 0

    h, xn = _embed(x, meta)
    w_out_bf = a_w_out.astype(BF16)
    w_o_bf = b_w_o.astype(BF16)
    w_down_bf = ffn_w_down.astype(BF16)
    k = v = nb = None
    for layer in range(n_a + n_b):
        if layer < n_a:
            g = _conv_in(xn, a_norm[layer], a_w_in, layer, a_conv[layer])
            h, xn = _res_norm(g, w_out_bf, layer, h, name="conv_out")
        else:
            if layer == n_a:
                k, v = _kv_proj(xn, kv_norm, w_kv, k_norm)
                nb = _key_bias(h, kv_norm, w_f, b_f, batch=batch, lp=lp)
            jb = layer - n_a
            q = _q_proj(xn, b_norm[jb], b_w_q, jb, b_q_norm[jb])
            o = _attention(q, k, v, nb, batch=batch, lp=lp, n_heads=n_heads)
            h, xn = _res_norm(o, w_o_bf, jb, h, name="attn_out")
        act = _ffn_up(xn, ffn_norm[layer], ffn_w_gu, layer)
        last = layer == n_a + n_b - 1
        h, xn = _res_norm(act, w_down_bf, layer, h, name="ffn_down",
                          unpad=(batch, lp, seq) if last else None)
    return h.reshape(batch, seq, d)
```
